```python
import jax, jax.numpy as jnp
from jax import lax
import numpy as np

D_MODEL = 4096
BATCH = 1
SEQ = 16384
DEPTH = 1
DEC_BATCH = 32
DEC_SEQ = 16
PAST_LEN = 4096

CHUNK = 64
LEFT_CHUNKS = 8
BAND_CHUNKS = LEFT_CHUNKS + 1
BAND_ROWS = LEFT_CHUNKS * CHUNK
MIX_ATT = D_MODEL // 2
MIX_CONV = D_MODEL - MIX_ATT
HEAD_DIM = 128
N_HEADS = MIX_ATT // HEAD_DIM
MAX_REL = 4 * CHUNK
CONV_WIDTH = 3
IN_COLS = 3 * MIX_ATT + 3 * MIX_CONV
N_EXPERTS = 32
TOP_K = 4
D_FF = D_MODEL
SWIGLU_LIMIT = 7.0
SWIGLU_ALPHA = 1.702
EPS = 1e-5

kernel_name = "hybrid_chunkband_shortconv_moe_stream_step"


def rmsnorm(x, g):
    xf = x.astype(jnp.float32)
    y = xf * lax.rsqrt(jnp.mean(xf * xf, axis=-1, keepdims=True) + EPS)
    return (y * g.astype(jnp.float32)).astype(x.dtype)


def split_projection(h, w_in):
    b, t = h.shape[:2]
    p = jnp.einsum('btd,de->bte', h, w_in)
    q, k, v, bg, cg, u = jnp.split(
        p, [MIX_ATT, 2 * MIX_ATT, 3 * MIX_ATT, 3 * MIX_ATT + MIX_CONV, 3 * MIX_ATT + 2 * MIX_CONV], axis=-1)
    heads = lambda a: a.reshape(b, t, N_HEADS, HEAD_DIM)
    return heads(q), heads(k), heads(v), bg, cg, u


def rel_bias(rel, table):
    return table[:, jnp.clip(rel, -MAX_REL, MAX_REL) + MAX_REL]


def band_attention_prompt(q, k, v, table):
    b, s, h, dh = q.shape
    nc = s // CHUNK
    pad = ((0, 0), (BAND_ROWS, 0), (0, 0), (0, 0))
    kp = jnp.pad(k, pad).reshape(b, nc + LEFT_CHUNKS, CHUNK, h, dh)
    vp = jnp.pad(v, pad).reshape(b, nc + LEFT_CHUNKS, CHUNK, h, dh)
    band_idx = jnp.arange(nc)[:, None] + jnp.arange(BAND_CHUNKS)[None, :]
    kb = kp[:, band_idx].reshape(b, nc, BAND_CHUNKS * CHUNK, h, dh)
    vb = vp[:, band_idx].reshape(b, nc, BAND_CHUNKS * CHUNK, h, dh)
    qc = q.reshape(b, nc, CHUNK, h, dh)
    i = jnp.arange(CHUNK)
    j = jnp.arange(BAND_CHUNKS * CHUNK)
    bias = rel_bias(BAND_ROWS + i[:, None] - j[None, :], table).astype(jnp.float32)
    key_pos = (jnp.arange(nc)[:, None] - LEFT_CHUNKS) * CHUNK + j[None, :]
    valid = key_pos >= 0
    scores = jnp.einsum('bnqhd,bnkhd->bnhqk', qc, kb).astype(jnp.float32) * (HEAD_DIM ** -0.5)
    scores = scores + bias[None, None]
    scores = jnp.where(valid[None, :, None, None, :], scores, -jnp.inf)
    probs = jax.nn.softmax(scores, axis=-1).astype(v.dtype)
    out = jnp.einsum('bnhqk,bnkhd->bnqhd', probs, vb)
    return out.reshape(b, s, h * dh)


def band_attention_sample(q, k_new, v_new, k_cache, v_cache, table):
    b, t, h, dh = q.shape
    r = k_cache.shape[1]
    kk = jnp.concatenate([k_cache.astype(k_new.dtype), k_new], axis=1)
    vv = jnp.concatenate([v_cache.astype(v_new.dtype), v_new], axis=1)
    rel = (r + jnp.arange(t))[:, None] - jnp.arange(r + t)[None, :]
    bias = rel_bias(rel, table).astype(jnp.float32)
    scores = jnp.einsum('bqhd,bkhd->bhqk', q, kk).astype(jnp.float32) * (HEAD_DIM ** -0.5)
    probs = jax.nn.softmax(scores + bias[None], axis=-1).astype(vv.dtype)
    out = jnp.einsum('bhqk,bkhd->bqhd', probs, vv)
    return out.reshape(b, t, h * dh)


def short_gated_conv(bg, cg, u, state, conv_w):
    t = u.shape[1]
    z = cg * u
    zp = jnp.concatenate([state.astype(z.dtype), z], axis=1)
    y = conv_w[0] * zp[:, 0:t]
    for j in range(1, CONV_WIDTH):
        y = y + conv_w[j] * zp[:, j:j + t]
    return bg * y, zp[:, -(CONV_WIDTH - 1):]


def mixer_sublayer(x, k_cache, v_cache, conv_state, norm_attn, w_in, conv_w, rel_table,
                   norm_att_out, norm_conv_out, w_out):
    h = rmsnorm(x, norm_attn)
    q, k, v, bg, cg, u = split_projection(h, w_in)
    if k_cache is None:
        att = band_attention_prompt(q, k, v, rel_table)
        rows = min(BAND_ROWS, x.shape[1])
        k_state, v_state = k[:, -rows:], v[:, -rows:]
        conv_state = jnp.zeros((x.shape[0], CONV_WIDTH - 1, MIX_CONV), dtype=u.dtype)
    else:
        att = band_attention_sample(q, k, v, k_cache, v_cache, rel_table)
        k_state, v_state = k, v
    conv_out, new_conv = short_gated_conv(bg, cg, u, conv_state, conv_w)
    merged = jnp.concatenate([rmsnorm(att, norm_att_out), rmsnorm(conv_out, norm_conv_out)], axis=-1)
    y = x + jnp.einsum('btm,md->btd', merged, w_out)
    return y, k_state, v_state, new_conv


def moe_ffn(h, w_router, b_router, w_gate_up, b_gate_up, w_down, b_down):
    b, t, d = h.shape
    hf = h.reshape(b * t, d)
    logits = (hf @ w_router).astype(jnp.float32) + b_router.astype(jnp.float32)
    top_val, top_idx = lax.top_k(logits, TOP_K)
    gates = jax.nn.softmax(top_val, axis=-1)
    dense_gate = jnp.sum(jax.nn.one_hot(top_idx, N_EXPERTS, dtype=jnp.float32) * gates[..., None],
                         axis=1).astype(h.dtype)
    y = jnp.zeros_like(hf)
    for e in range(N_EXPERTS):
        gu = hf @ w_gate_up[e] + b_gate_up[e]
        gate = jnp.minimum(gu[:, :D_FF], SWIGLU_LIMIT)
        up = jnp.clip(gu[:, D_FF:], -SWIGLU_LIMIT, SWIGLU_LIMIT)
        act = (up + 1.0) * (gate * jax.nn.sigmoid(SWIGLU_ALPHA * gate))
        y = y + dense_gate[:, e:e + 1] * (act @ w_down[e] + b_down[e])
    return y.reshape(b, t, d)


def setup_inputs(seed: int = 0) -> dict:
    key = jax.random.key(seed)
    ks = jax.random.split(key, 20)
    f32 = jnp.float32
    r = min(BAND_ROWS, PAST_LEN)
    nrm = lambda k, shape, scale: jax.random.normal(k, shape, dtype=f32) * scale
    gain = lambda k, shape: 1.0 + 0.02 * jax.random.normal(k, shape, dtype=f32)
    return {
        "x_prompt": nrm(ks[0], (BATCH, SEQ, D_MODEL), 1.0),
        "x_sample": nrm(ks[1], (DEC_BATCH, DEC_SEQ, D_MODEL), 1.0),
        "cache_attn_k": nrm(ks[2], (DEPTH, DEC_BATCH, r, N_HEADS, HEAD_DIM), 1.0),
        "cache_attn_v": nrm(ks[3], (DEPTH, DEC_BATCH, r, N_HEADS, HEAD_DIM), 1.0),
        "state_conv": nrm(ks[4], (DEPTH, DEC_BATCH, CONV_WIDTH - 1, MIX_CONV), 1.0),
        "norm_attn": gain(ks[5], (DEPTH, D_MODEL)),
        "w_in": nrm(ks[6], (DEPTH, D_MODEL, IN_COLS), D_MODEL ** -0.5),
        "conv_w": nrm(ks[7], (DEPTH, CONV_WIDTH, MIX_CONV), CONV_WIDTH ** -0.5),
        "rel_table": nrm(ks[8], (DEPTH, N_HEADS, 2 * MAX_REL + 1), 0.1),
        "norm_att_out": gain(ks[9], (DEPTH, MIX_ATT)),
        "norm_conv_out": gain(ks[10], (DEPTH, MIX_CONV)),
        "w_out": nrm(ks[11], (DEPTH, D_MODEL, D_MODEL), D_MODEL ** -0.5),
        "norm_ffn": gain(ks[12], (DEPTH, D_MODEL)),
        "w_router": nrm(ks[13], (DEPTH, D_MODEL, N_EXPERTS), D_MODEL ** -0.5),
        "b_router": nrm(ks[14], (DEPTH, N_EXPERTS), 0.01),
        "w_gate_up": nrm(ks[15], (DEPTH, N_EXPERTS, D_MODEL, 2 * D_FF), D_MODEL ** -0.5),
        "b_gate_up": nrm(ks[16], (DEPTH, N_EXPERTS, 2 * D_FF), 0.01),
        "w_down": nrm(ks[17], (DEPTH, N_EXPERTS, D_FF, D_MODEL), D_FF ** -0.5),
        "b_down": nrm(ks[18], (DEPTH, N_EXPERTS, D_MODEL), 0.01),
        "norm_final": gain(ks[19], (D_MODEL,)),
    }


def reference(x_prompt, x_sample, cache_attn_k, cache_attn_v, state_conv, norm_attn, w_in, conv_w,
              rel_table, norm_att_out, norm_conv_out, w_out, norm_ffn, w_router, b_router,
              w_gate_up, b_gate_up, w_down, b_down, norm_final):
    yp, ys = x_prompt, x_sample
    kp_l, vp_l, cp_l, ks_l, vs_l, cs_l = [], [], [], [], [], []
    for l in range(DEPTH):
        mix = (norm_attn[l], w_in[l], conv_w[l], rel_table[l], norm_att_out[l], norm_conv_out[l], w_out[l])
        ffn = (w_router[l], b_router[l], w_gate_up[l], b_gate_up[l], w_down[l], b_down[l])
        yp, kp, vp, cp = mixer_sublayer(yp, None, None, None, *mix)
        yp = yp + moe_ffn(rmsnorm(yp, norm_ffn[l]), *ffn)
        ys, k_new, v_new, cs = mixer_sublayer(ys, cache_attn_k[l], cache_attn_v[l], state_conv[l], *mix)
        ys = ys + moe_ffn(rmsnorm(ys, norm_ffn[l]), *ffn)
        kp_l.append(kp); vp_l.append(vp); cp_l.append(cp)
        ks_l.append(k_new); vs_l.append(v_new); cs_l.append(cs)
    y_prompt = rmsnorm(yp, norm_final)
    y_sample = rmsnorm(ys, norm_final)
    return (y_prompt, y_sample, jnp.stack(kp_l), jnp.stack(vp_l), jnp.stack(cp_l),
            jnp.stack(ks_l), jnp.stack(vs_l), jnp.stack(cs_l))
```

```python
import functools

import jax
import jax.numpy as jnp
from jax import lax
from jax.experimental import pallas as pl
from jax.experimental.pallas import tpu as pltpu

D_MODEL = 4096
SEQ = 16384
DEC_BATCH = 32
DEC_SEQ = 16
CHUNK = 64
LEFT_CHUNKS = 8
BAND_ROWS = LEFT_CHUNKS * CHUNK
MIX_ATT = 2048
MIX_CONV = 2048
HEAD_DIM = 128
N_HEADS = 16
MAX_REL = 4 * CHUNK
IN_COLS = 3 * MIX_ATT + 3 * MIX_CONV
N_EXPERTS = 32
TOP_K = 4
D_FF = D_MODEL
SWIGLU_LIMIT = 7.0
SWIGLU_ALPHA = 1.702
EPS = 1e-5
SCALE = HEAD_DIM ** -0.5

N_PROMPT = SEQ
N_SAMPLE = DEC_BATCH * DEC_SEQ
N_TOK = N_PROMPT + N_SAMPLE
N_PAIRS = N_TOK * TOP_K

LANES = 128
ROW_TILE = 512
N_ROW_TILES = N_TOK // ROW_TILE
N_PROMPT_TILES = N_PROMPT // ROW_TILE
HALO_ROWS = 8
MIX_TILE = 256
MIX_PROMPT_TILES = N_PROMPT // MIX_TILE

MOE_TILE = 512
MOE_ROWS = N_PAIRS + N_EXPERTS * MOE_TILE
MOE_TILES = MOE_ROWS // MOE_TILE
FF_TILE = 512
COMBINE_TILE = 128

VMEM_LIMIT = 56 * 1024 * 1024

BF16 = jnp.bfloat16
F32 = jnp.float32
NEG_INF = float("-inf")


def _params(*sem):
    return pltpu.CompilerParams(dimension_semantics=sem, vmem_limit_bytes=VMEM_LIMIT)


def _rms(x, g):
    return x * lax.rsqrt(jnp.mean(x * x, axis=-1, keepdims=True) + EPS) * g


def _norm_matmul_kernel(x_ref, g_ref, w_ref, o_ref, h_ref):
    @pl.when(pl.program_id(1) == 0)
    def _():
        h_ref[...] = _rms(x_ref[...], g_ref[...]).astype(BF16)

    o_ref[...] = jnp.dot(h_ref[...], w_ref[...], preferred_element_type=F32)


def _norm_matmul(x, g, w, tn):
    m, k = x.shape
    n = w.shape[1]
    return pl.pallas_call(
        _norm_matmul_kernel,
        grid=(m // ROW_TILE, n // tn),
        in_specs=[
            pl.BlockSpec((ROW_TILE, k), lambda i, j: (i, 0)),
            pl.BlockSpec((1, k), lambda i, j: (0, 0)),
            pl.BlockSpec((k, tn), lambda i, j: (0, j)),
        ],
        out_specs=pl.BlockSpec((ROW_TILE, tn), lambda i, j: (i, j)),
        out_shape=jax.ShapeDtypeStruct((m, n), F32),
        scratch_shapes=[pltpu.VMEM((ROW_TILE, k), BF16)],
        compiler_params=_params("parallel", "arbitrary"),
        name="norm_in_proj",
    )(x, g, w)


def _attn_prompt_kernel(q_ref, kp_ref, kc_ref, vp_ref, vc_ref, b_ref, o_ref):
    m = pl.program_id(1)
    q = q_ref[...].astype(BF16)
    k = jnp.concatenate([kp_ref[...], kc_ref[...]], axis=0).astype(BF16)
    v = jnp.concatenate([vp_ref[...], vc_ref[...]], axis=0).astype(BF16)
    s = lax.dot_general(q, k, (((1,), (1,)), ((), ())), preferred_element_type=F32)
    s = s * SCALE + b_ref[0]
    col = lax.broadcasted_iota(jnp.int32, s.shape, 1)
    s = jnp.where(col < jnp.where(m == 0, ROW_TILE, 0), NEG_INF, s)
    e = jnp.exp(s - jnp.max(s, axis=-1, keepdims=True))
    p = (e / jnp.sum(e, axis=-1, keepdims=True)).astype(BF16)
    o_ref[...] = jnp.dot(p, v, preferred_element_type=F32)


def _attn_prompt(p_all, bias):
    blk = (ROW_TILE, HEAD_DIM)
    prev = lambda h, m: jnp.maximum(m - 1, 0)
    return pl.pallas_call(
        _attn_prompt_kernel,
        grid=(N_HEADS, N_PROMPT_TILES),
        in_specs=[
            pl.BlockSpec(blk, lambda h, m: (m, h)),
            pl.BlockSpec(blk, lambda h, m: (prev(h, m), N_HEADS + h)),
            pl.BlockSpec(blk, lambda h, m: (m, N_HEADS + h)),
            pl.BlockSpec(blk, lambda h, m: (prev(h, m), 2 * N_HEADS + h)),
            pl.BlockSpec(blk, lambda h, m: (m, 2 * N_HEADS + h)),
            pl.BlockSpec((1, ROW_TILE, 2 * ROW_TILE), lambda h, m: (h, 0, 0)),
        ],
        out_specs=pl.BlockSpec(blk, lambda h, m: (m, h)),
        out_shape=jax.ShapeDtypeStruct((N_PROMPT, MIX_ATT), F32),
        compiler_params=_params("parallel", "arbitrary"),
        name="attn_prompt",
    )(p_all, p_all, p_all, p_all, p_all, bias)


def _prompt_bias(table):
    i = jnp.arange(ROW_TILE)[:, None]
    j = jnp.arange(2 * ROW_TILE)[None, :]
    rel = jnp.clip(ROW_TILE + i - j, -MAX_REL, MAX_REL) + MAX_REL
    lag = (LEFT_CHUNKS + i // CHUNK) - j // CHUNK
    band = (lag >= 0) & (lag <= LEFT_CHUNKS)
    return jnp.where(band[None], table[:, rel], NEG_INF).astype(F32)


def _attn_sample_kernel(q_ref, k_ref, v_ref, ck_ref, cv_ref, bc_ref, bn_ref, o_ref):
    tb = (((1,), (1,)), ((), ()))
    for h in range(N_HEADS):
        sl = slice(h * HEAD_DIM, (h + 1) * HEAD_DIM)
        q = q_ref[:, sl].astype(BF16)
        kn = k_ref[:, sl].astype(BF16)
        vn = v_ref[:, sl].astype(BF16)
        kc = ck_ref[0, :, sl].astype(BF16)
        vc = cv_ref[0, :, sl].astype(BF16)
        sc = lax.dot_general(q, kc, tb, preferred_element_type=F32) * SCALE + bc_ref[h]
        sn = lax.dot_general(q, kn, tb, preferred_element_type=F32) * SCALE + bn_ref[h]
        mx = jnp.maximum(jnp.max(sc, axis=-1, keepdims=True), jnp.max(sn, axis=-1, keepdims=True))
        ec = jnp.exp(sc - mx)
        en = jnp.exp(sn - mx)
        tot = jnp.sum(ec, axis=-1, keepdims=True) + jnp.sum(en, axis=-1, keepdims=True)
        o_ref[:, sl] = (jnp.dot((ec / tot).astype(BF16), vc, preferred_element_type=F32)
                        + jnp.dot((en / tot).astype(BF16), vn, preferred_element_type=F32))


def _attn_sample(p_all, cache_k, cache_v, bias_c, bias_n):
    r = cache_k.shape[1]
    first = N_PROMPT // DEC_SEQ
    new = lambda c: pl.BlockSpec((DEC_SEQ, MIX_ATT), lambda b: (first + b, c))
    cache = pl.BlockSpec((1, r, MIX_ATT), lambda b: (b, 0, 0))
    return pl.pallas_call(
        _attn_sample_kernel,
        grid=(DEC_BATCH,),
        in_specs=[new(0), new(1), new(2), cache, cache,
                  pl.BlockSpec((N_HEADS, DEC_SEQ, r), lambda b: (0, 0, 0)),
                  pl.BlockSpec((N_HEADS, DEC_SEQ, DEC_SEQ), lambda b: (0, 0, 0))],
        out_specs=pl.BlockSpec((DEC_SEQ, MIX_ATT), lambda b: (b, 0)),
        out_shape=jax.ShapeDtypeStruct((N_SAMPLE, MIX_ATT), F32),
        compiler_params=_params("parallel"),
        name="attn_sample",
    )(p_all, p_all, p_all, cache_k, cache_v, bias_c, bias_n)


def _sample_bias(table, r):
    rel = (r + jnp.arange(DEC_SEQ))[:, None] - jnp.arange(r + DEC_SEQ)[None, :]
    bias = table[:, jnp.clip(rel, -MAX_REL, MAX_REL) + MAX_REL].astype(F32)
    return bias[:, :, :r], bias[:, :, r:]


def _mix_kernel(ap_ref, as_ref, bg_ref, cg_ref, u_ref, cgh_ref, uh_ref, s1_ref, s2_ref,
                cw_ref, ga_ref, gc_ref, o_ref, z_ref):
    i = pl.program_id(0)
    sample = i >= MIX_PROMPT_TILES
    z = cg_ref[...] * u_ref[...]
    z_ref[...] = z
    zh = jnp.where(i > 0, cgh_ref[...] * uh_ref[...], 0.0)
    row = lax.broadcasted_iota(jnp.int32, z.shape, 0)
    pos = jnp.where(sample, row & (DEC_SEQ - 1), row)
    z1 = jnp.where(pos == 0, jnp.where(sample, s1_ref[...], zh[7:8]), pltpu.roll(z, 1, 0))
    z2 = jnp.where(pos < 2,
                   jnp.where(sample, s2_ref[...], jnp.where(row == 0, zh[6:7], zh[7:8])),
                   pltpu.roll(z, 2, 0))
    y = cw_ref[0:1] * z2
    y = y + cw_ref[1:2] * z1
    y = y + cw_ref[2:3] * z
    att = jnp.where(sample, as_ref[...], ap_ref[...])
    o_ref[:, :MIX_ATT] = _rms(att, ga_ref[...]).astype(BF16)
    o_ref[:, MIX_ATT:] = _rms(bg_ref[...] * y, gc_ref[...]).astype(BF16)


def _mix(att_p, att_s, p_all, s1, s2, conv_w, g_att, g_conv):
    tile = lambda c: pl.BlockSpec((MIX_TILE, MIX_CONV), lambda i: (i, c))
    halo = lambda c: pl.BlockSpec(
        (HALO_ROWS, MIX_CONV), lambda i: (jnp.maximum(i * (MIX_TILE // HALO_ROWS) - 1, 0), c))
    const = lambda shape: pl.BlockSpec(shape, lambda i: (0, 0))
    sample_tile = lambda: pl.BlockSpec((MIX_TILE, MIX_CONV), lambda i: (jnp.maximum(i - MIX_PROMPT_TILES, 0), 0))
    return pl.pallas_call(
        _mix_kernel,
        grid=(N_TOK // MIX_TILE,),
        in_specs=[
            pl.BlockSpec((MIX_TILE, MIX_ATT), lambda i: (jnp.minimum(i, MIX_PROMPT_TILES - 1), 0)),
            sample_tile(),
            tile(3), tile(4), tile(5), halo(4), halo(5),
            sample_tile(), sample_tile(),
            const((3, MIX_CONV)), const((1, MIX_ATT)), const((1, MIX_CONV)),
        ],
        out_specs=[pl.BlockSpec((MIX_TILE, D_MODEL), lambda i: (i, 0)),
                   pl.BlockSpec((MIX_TILE, MIX_CONV), lambda i: (i, 0))],
        out_shape=[jax.ShapeDtypeStruct((N_TOK, D_MODEL), BF16),
                   jax.ShapeDtypeStruct((N_TOK, MIX_CONV), F32)],
        compiler_params=_params("parallel"),
        name="conv_norms",
    )(att_p, att_s, p_all, p_all, p_all, p_all, p_all, s1, s2, conv_w, g_att, g_conv)


def _matmul_residual_kernel(a_ref, w_ref, r_ref, o_ref):
    o_ref[...] = r_ref[...] + jnp.dot(a_ref[...], w_ref[...], preferred_element_type=F32)


def _matmul_residual(a, w, res, tn):
    m, k = a.shape
    n = w.shape[1]
    return pl.pallas_call(
        _matmul_residual_kernel,
        grid=(m // ROW_TILE, n // tn),
        in_specs=[
            pl.BlockSpec((ROW_TILE, k), lambda i, j: (i, 0)),
            pl.BlockSpec((k, tn), lambda i, j: (0, j)),
            pl.BlockSpec((ROW_TILE, tn), lambda i, j: (i, j)),
        ],
        out_specs=pl.BlockSpec((ROW_TILE, tn), lambda i, j: (i, j)),
        out_shape=jax.ShapeDtypeStruct((m, n), F32),
        compiler_params=_params("parallel", "arbitrary"),
        name="out_proj",
    )(a, w, res)


def _router_kernel(y_ref, g_ref, w_ref, b_ref, h_ref, idx_ref, gate_ref):
    h = _rms(y_ref[...], g_ref[...])
    h_ref[...] = h
    logits = jnp.dot(h.astype(BF16), w_ref[...], preferred_element_type=F32) + b_ref[...]
    lane = lax.broadcasted_iota(jnp.int32, logits.shape, 1).astype(F32)
    left = jnp.where(lane < N_EXPERTS, logits, NEG_INF)
    vals, idxs = [], []
    for _ in range(TOP_K):
        top = jnp.max(left, axis=-1, keepdims=True)
        pick = jnp.min(jnp.where(left == top, lane, float(LANES)), axis=-1, keepdims=True)
        vals.append(top)
        idxs.append(pick)
        left = jnp.where(lane == pick, NEG_INF, left)
    exps = [jnp.exp(v - vals[0]) for v in vals]
    tot = exps[0]
    for e in exps[1:]:
        tot = tot + e
    idx_out = jnp.zeros_like(logits)
    gate_out = jnp.zeros_like(logits)
    for k in range(TOP_K):
        idx_out = jnp.where(lane == k, idxs[k], idx_out)
        gate_out = jnp.where(lane == k, exps[k] / tot, gate_out)
    idx_ref[...] = idx_out.astype(jnp.int32)
    gate_ref[...] = gate_out


def _router(y, g, w, b):
    const = lambda shape: pl.BlockSpec(shape, lambda i: (0, 0))
    return pl.pallas_call(
        _router_kernel,
        grid=(N_ROW_TILES,),
        in_specs=[pl.BlockSpec((ROW_TILE, D_MODEL), lambda i: (i, 0)),
                  const((1, D_MODEL)), const((D_MODEL, LANES)), const((1, LANES))],
        out_specs=[pl.BlockSpec((ROW_TILE, D_MODEL), lambda i: (i, 0)),
                   pl.BlockSpec((ROW_TILE, LANES), lambda i: (i, 0)),
                   pl.BlockSpec((ROW_TILE, LANES), lambda i: (i, 0))],
        out_shape=[jax.ShapeDtypeStruct((N_TOK, D_MODEL), F32),
                   jax.ShapeDtypeStruct((N_TOK, LANES), jnp.int32),
                   jax.ShapeDtypeStruct((N_TOK, LANES), F32)],
        compiler_params=_params("parallel"),
        name="ffn_norm_router",
    )(y, g, w, b)


def _routing(top_idx):
    e_flat = top_idx.reshape(-1)
    onehot = (e_flat[:, None] == jnp.arange(N_EXPERTS)[None, :]).astype(jnp.int32)
    csum = jnp.cumsum(onehot, axis=0)
    rank = jnp.take_along_axis(csum, e_flat[:, None], axis=1)[:, 0] - 1
    count = csum[-1]
    padded = ((count + MOE_TILE - 1) // MOE_TILE) * MOE_TILE
    group_end = jnp.cumsum(padded)
    pos = (group_end - padded)[e_flat] + rank
    src = jnp.zeros((MOE_ROWS,), jnp.int32).at[pos].set(jnp.arange(N_PAIRS, dtype=jnp.int32) // TOP_K)
    n_active = group_end[-1] // MOE_TILE
    tile = jnp.arange(MOE_TILES, dtype=jnp.int32)
    tile_expert = jnp.searchsorted(group_end, jnp.minimum(tile, n_active - 1) * MOE_TILE, side="right")
    return pos.astype(jnp.int32), src, tile_expert.astype(jnp.int32), n_active.astype(jnp.int32).reshape(1)


def _row_copy(src_hbm, row, buf, slot, sem):
    return pltpu.make_async_copy(src_hbm.at[pl.ds(row, 1), :], buf.at[pl.ds(slot, 1), :], sem)


def _gather_kernel(src_ref, h_hbm, o_ref, buf, sem):
    def start(r, c):
        _row_copy(h_hbm, src_ref[0, 0, r], buf, r, sem).start()
        return c

    def wait(r, c):
        _row_copy(h_hbm, 0, buf, r, sem).wait()
        return c

    lax.fori_loop(0, MOE_TILE, start, 0)
    lax.fori_loop(0, MOE_TILE, wait, 0)
    o_ref[...] = buf[...].astype(BF16)


def _gather(src, h):
    return pl.pallas_call(
        _gather_kernel,
        grid=(MOE_TILES,),
        in_specs=[pl.BlockSpec((1, 1, MOE_TILE), lambda i: (i, 0, 0), memory_space=pltpu.SMEM),
                  pl.BlockSpec(memory_space=pl.ANY)],
        out_specs=pl.BlockSpec((MOE_TILE, D_MODEL), lambda i: (i, 0)),
        out_shape=jax.ShapeDtypeStruct((MOE_ROWS, D_MODEL), BF16),
        scratch_shapes=[pltpu.VMEM((MOE_TILE, D_MODEL), F32), pltpu.SemaphoreType.DMA(())],
        compiler_params=_params("arbitrary"),
        name="gather_rows",
    )(src.reshape(MOE_TILES, 1, MOE_TILE), h)


def _expert_changed(te_ref, i):
    return (i == 0) | (te_ref[i] != te_ref[jnp.maximum(i - 1, 0)])


def _gate_up_kernel(te_ref, na_ref, x_ref, wg_ref, wu_ref, bg_ref, bu_ref, o_ref, wg_bf, wu_bf):
    i = pl.program_id(1)

    @pl.when(_expert_changed(te_ref, i))
    def _():
        wg_bf[...] = wg_ref[...].astype(BF16)
        wu_bf[...] = wu_ref[...].astype(BF16)

    @pl.when(i < na_ref[0])
    def _():
        x = x_ref[...]
        gate = jnp.dot(x, wg_bf[...], preferred_element_type=F32) + bg_ref[...]
        up = jnp.dot(x, wu_bf[...], preferred_element_type=F32) + bu_ref[...]
        gate = jnp.minimum(gate, SWIGLU_LIMIT)
        up = jnp.clip(up, -SWIGLU_LIMIT, SWIGLU_LIMIT)
        o_ref[...] = ((up + 1.0) * (gate * jax.nn.sigmoid(SWIGLU_ALPHA * gate))).astype(BF16)

    @pl.when(i >= na_ref[0])
    def _():
        o_ref[...] = jnp.zeros_like(o_ref)


def _gate_up(tile_expert, n_active, xs, w, b):
    n_ff = D_FF // FF_TILE
    wspec = lambda off: pl.BlockSpec((None, D_MODEL, FF_TILE), lambda j, i, te, na: (te[i], 0, off + j))
    bspec = lambda off: pl.BlockSpec((None, 1, FF_TILE), lambda j, i, te, na: (te[i], 0, off + j))
    return pl.pallas_call(
        _gate_up_kernel,
        grid_spec=pltpu.PrefetchScalarGridSpec(
            num_scalar_prefetch=2,
            grid=(n_ff, MOE_TILES),
            in_specs=[pl.BlockSpec((MOE_TILE, D_MODEL), lambda j, i, te, na: (i, 0)),
                      wspec(0), wspec(n_ff), bspec(0), bspec(n_ff)],
            out_specs=pl.BlockSpec((MOE_TILE, FF_TILE), lambda j, i, te, na: (i, j)),
            scratch_shapes=[pltpu.VMEM((D_MODEL, FF_TILE), BF16), pltpu.VMEM((D_MODEL, FF_TILE), BF16)],
        ),
        out_shape=jax.ShapeDtypeStruct((MOE_ROWS, D_FF), BF16),
        compiler_params=_params("arbitrary", "arbitrary"),
        name="expert_gate_up",
    )(tile_expert, n_active, xs, w, w, b, b)


def _down_kernel(te_ref, na_ref, x_ref, w_ref, b_ref, o_ref, w_bf):
    i = pl.program_id(1)

    @pl.when(_expert_changed(te_ref, i))
    def _():
        w_bf[...] = w_ref[...].astype(BF16)

    @pl.when(i < na_ref[0])
    def _():
        o_ref[...] = jnp.dot(x_ref[...], w_bf[...], preferred_element_type=F32) + b_ref[...]

    @pl.when(i >= na_ref[0])
    def _():
        o_ref[...] = jnp.zeros_like(o_ref)


def _down(tile_expert, n_active, hs, w, b):
    return pl.pallas_call(
        _down_kernel,
        grid_spec=pltpu.PrefetchScalarGridSpec(
            num_scalar_prefetch=2,
            grid=(D_MODEL // FF_TILE, MOE_TILES),
            in_specs=[pl.BlockSpec((MOE_TILE, D_FF), lambda j, i, te, na: (i, 0)),
                      pl.BlockSpec((None, D_FF, FF_TILE), lambda j, i, te, na: (te[i], 0, j)),
                      pl.BlockSpec((None, 1, FF_TILE), lambda j, i, te, na: (te[i], 0, j))],
            out_specs=pl.BlockSpec((MOE_TILE, FF_TILE), lambda j, i, te, na: (i, j)),
            scratch_shapes=[pltpu.VMEM((D_FF, FF_TILE), BF16)],
        ),
        out_shape=jax.ShapeDtypeStruct((MOE_ROWS, D_MODEL), F32),
        compiler_params=_params("arbitrary", "arbitrary"),
        name="expert_down",
    )(tile_expert, n_active, hs, w, b)


def _combine_kernel(pos_ref, y_ref, gate_ref, g_ref, ys_hbm, o_ref, buf, sem):
    def start(t, c):
        for k in range(TOP_K):
            _row_copy(ys_hbm, pos_ref[0, 0, t * TOP_K + k], buf, k * COMBINE_TILE + t, sem).start()
        return c

    def wait(r, c):
        _row_copy(ys_hbm, 0, buf, r, sem).wait()
        return c

    lax.fori_loop(0, COMBINE_TILE, start, 0)
    lax.fori_loop(0, TOP_K * COMBINE_TILE, wait, 0)
    gates = gate_ref[...]
    moe = gates[:, 0:1] * buf[0:COMBINE_TILE, :]
    for k in range(1, TOP_K):
        moe = moe + gates[:, k:k + 1] * buf[k * COMBINE_TILE:(k + 1) * COMBINE_TILE, :]
    o_ref[...] = _rms(y_ref[...] + moe, g_ref[...])


def _combine(pos, y, gates, g, ys):
    n_tiles = N_TOK // COMBINE_TILE
    return pl.pallas_call(
        _combine_kernel,
        grid=(n_tiles,),
        in_specs=[pl.BlockSpec((1, 1, TOP_K * COMBINE_TILE), lambda i: (i, 0, 0), memory_space=pltpu.SMEM),
                  pl.BlockSpec((COMBINE_TILE, D_MODEL), lambda i: (i, 0)),
                  pl.BlockSpec((COMBINE_TILE, LANES), lambda i: (i, 0)),
                  pl.BlockSpec((1, D_MODEL), lambda i: (0, 0)),
                  pl.BlockSpec(memory_space=pl.ANY)],
        out_specs=pl.BlockSpec((COMBINE_TILE, D_MODEL), lambda i: (i, 0)),
        out_shape=jax.ShapeDtypeStruct((N_TOK, D_MODEL), F32),
        scratch_shapes=[pltpu.VMEM((TOP_K * COMBINE_TILE, D_MODEL), F32), pltpu.SemaphoreType.DMA(())],
        compiler_params=_params("arbitrary"),
        name="combine_final_norm",
    )(pos.reshape(n_tiles, 1, TOP_K * COMBINE_TILE), y, gates, g, ys)


def kernel(x_prompt, x_sample, cache_attn_k, cache_attn_v, state_conv, norm_attn, w_in, conv_w,
           rel_table, norm_att_out, norm_conv_out, w_out, norm_ffn, w_router, b_router,
           w_gate_up, b_gate_up, w_down, b_down, norm_final):
    row = lambda g: g.reshape(1, -1)
    x = jnp.concatenate([x_prompt.reshape(N_PROMPT, D_MODEL), x_sample.reshape(N_SAMPLE, D_MODEL)], axis=0)

    p_all = _norm_matmul(x, row(norm_attn[0]), w_in[0].astype(BF16), 1024)

    r = cache_attn_k.shape[2]
    att_p = _attn_prompt(p_all, _prompt_bias(rel_table[0]))
    bias_c, bias_n = _sample_bias(rel_table[0], r)
    att_s = _attn_sample(p_all, cache_attn_k[0].reshape(DEC_BATCH, r, MIX_ATT),
                         cache_attn_v[0].reshape(DEC_BATCH, r, MIX_ATT), bias_c, bias_n)

    st = state_conv[0]
    pad = lambda a: jnp.pad(a, ((0, 0), (0, DEC_SEQ - a.shape[1]), (0, 0))).reshape(N_SAMPLE, MIX_CONV)
    merged, z = _mix(att_p, att_s, p_all, pad(st[:, 1:2]), pad(st), conv_w[0],
                     row(norm_att_out[0]), row(norm_conv_out[0]))
    y1 = _matmul_residual(merged, w_out[0].astype(BF16), x, 1024)

    w_r = jnp.pad(w_router[0], ((0, 0), (0, LANES - N_EXPERTS))).astype(BF16)
    b_r = jnp.pad(b_router[0], (0, LANES - N_EXPERTS)).reshape(1, LANES)
    h, idx, gates = _router(y1, row(norm_ffn[0]), w_r, b_r)

    pos, src, tile_expert, n_active = _routing(idx[:, :TOP_K])
    xs = _gather(src, h)
    hs = _gate_up(tile_expert, n_active, xs, w_gate_up.reshape(N_EXPERTS, D_MODEL, 2 * D_FF),
                  b_gate_up.reshape(N_EXPERTS, 1, 2 * D_FF))
    ys = _down(tile_expert, n_active, hs, w_down.reshape(N_EXPERTS, D_FF, D_MODEL),
               b_down.reshape(N_EXPERTS, 1, D_MODEL))
    y = _combine(pos, y1, gates, row(norm_final), ys)

    k_all = p_all[:, MIX_ATT:2 * MIX_ATT]
    v_all = p_all[:, 2 * MIX_ATT:3 * MIX_ATT]
    keep = min(BAND_ROWS, SEQ)
    shape_p = (1, 1, keep, N_HEADS, HEAD_DIM)
    shape_s = (1, DEC_BATCH, DEC_SEQ, N_HEADS, HEAD_DIM)
    z_s = z[N_PROMPT:].reshape(DEC_BATCH, DEC_SEQ, MIX_CONV)
    return (y[:N_PROMPT].reshape(1, SEQ, D_MODEL),
            y[N_PROMPT:].reshape(DEC_BATCH, DEC_SEQ, D_MODEL),
            k_all[N_PROMPT - keep:N_PROMPT].reshape(shape_p),
            v_all[N_PROMPT - keep:N_PROMPT].reshape(shape_p),
            z[N_PROMPT - 2:N_PROMPT].reshape(1, 1, 2, MIX_CONV),
            k_all[N_PROMPT:].reshape(shape_s),
            v_all[N_PROMPT:].reshape(shape_s),
            z_s[:, DEC_SEQ - 2:].reshape(1, DEC_BATCH, 2, MIX_CONV))
```

```python
import functools

import jax
import jax.numpy as jnp
import numpy as np
from jax import lax
from jax.experimental import pallas as pl
from jax.experimental.pallas import tpu as pltpu

D_MODEL = 4096
SEQ = 16384
DEC_BATCH = 32
DEC_SEQ = 16
CHUNK = 64
LEFT_CHUNKS = 8
BAND_ROWS = LEFT_CHUNKS * CHUNK
MIX_ATT = 2048
MIX_CONV = 2048
HEAD_DIM = 128
N_HEADS = 16
MAX_REL = 4 * CHUNK
IN_COLS = 3 * MIX_ATT + 3 * MIX_CONV
N_EXPERTS = 32
TOP_K = 4
D_FF = D_MODEL
SWIGLU_LIMIT = 7.0
SWIGLU_ALPHA = 1.702
EPS = 1e-5
SCALE = HEAD_DIM ** -0.5

N_PROMPT = SEQ
N_SAMPLE = DEC_BATCH * DEC_SEQ
N_TOK = N_PROMPT + N_SAMPLE
N_PAIRS = N_TOK * TOP_K

LANES = 128
ROW_TILE = 512
N_ROW_TILES = N_TOK // ROW_TILE
N_PROMPT_TILES = N_PROMPT // ROW_TILE
HALO_ROWS = 8
MIX_TILE = 256
MIX_PROMPT_TILES = N_PROMPT // MIX_TILE

MOE_TILE = 256
MOE_ROWS = N_PAIRS + N_EXPERTS * MOE_TILE
MOE_TILES = MOE_ROWS // MOE_TILE
EXPERT_COLS = 1024
WEIGHT_CHUNK = 512
STAGE_SLOTS = 3
DMA_UNROLL = 8
COMBINE_TILE = 128

VMEM_LIMIT = 56 * 1024 * 1024

BF16 = jnp.bfloat16
F32 = jnp.float32
NEG_INF = float("-inf")


def _params(*sem):
    return pltpu.CompilerParams(dimension_semantics=sem, vmem_limit_bytes=VMEM_LIMIT)


def _rms(x, g):
    return x * lax.rsqrt(jnp.mean(x * x, axis=-1, keepdims=True) + EPS) * g


def _norm_matmul_kernel(x_ref, g_ref, w_ref, o_ref, h_ref):
    @pl.when(pl.program_id(1) == 0)
    def _():
        h_ref[...] = _rms(x_ref[...], g_ref[...]).astype(BF16)

    o_ref[...] = jnp.dot(h_ref[...], w_ref[...], preferred_element_type=F32)


def _norm_matmul(x, g, w, tn):
    m, k = x.shape
    n = w.shape[1]
    return pl.pallas_call(
        _norm_matmul_kernel,
        grid=(m // ROW_TILE, n // tn),
        in_specs=[
            pl.BlockSpec((ROW_TILE, k), lambda i, j: (i, 0)),
            pl.BlockSpec((1, k), lambda i, j: (0, 0)),
            pl.BlockSpec((k, tn), lambda i, j: (0, j)),
        ],
        out_specs=pl.BlockSpec((ROW_TILE, tn), lambda i, j: (i, j)),
        out_shape=jax.ShapeDtypeStruct((m, n), F32),
        scratch_shapes=[pltpu.VMEM((ROW_TILE, k), BF16)],
        compiler_params=_params("parallel", "arbitrary"),
        name="norm_in_proj",
    )(x, g, w)


def _attn_prompt_kernel(q_ref, kp_ref, kc_ref, vp_ref, vc_ref, b_ref, o_ref):
    m = pl.program_id(1)
    q = q_ref[...].astype(BF16)
    k = jnp.concatenate([kp_ref[...], kc_ref[...]], axis=0).astype(BF16)
    v = jnp.concatenate([vp_ref[...], vc_ref[...]], axis=0).astype(BF16)
    s = lax.dot_general(q, k, (((1,), (1,)), ((), ())), preferred_element_type=F32)
    s = s * SCALE + b_ref[0]
    col = lax.broadcasted_iota(jnp.int32, s.shape, 1)
    s = jnp.where(col < jnp.where(m == 0, ROW_TILE, 0), NEG_INF, s)
    e = jnp.exp(s - jnp.max(s, axis=-1, keepdims=True))
    p = (e / jnp.sum(e, axis=-1, keepdims=True)).astype(BF16)
    o_ref[...] = jnp.dot(p, v, preferred_element_type=F32)


def _attn_prompt(p_all, bias):
    blk = (ROW_TILE, HEAD_DIM)
    prev = lambda h, m: jnp.maximum(m - 1, 0)
    return pl.pallas_call(
        _attn_prompt_kernel,
        grid=(N_HEADS, N_PROMPT_TILES),
        in_specs=[
            pl.BlockSpec(blk, lambda h, m: (m, h)),
            pl.BlockSpec(blk, lambda h, m: (prev(h, m), N_HEADS + h)),
            pl.BlockSpec(blk, lambda h, m: (m, N_HEADS + h)),
            pl.BlockSpec(blk, lambda h, m: (prev(h, m), 2 * N_HEADS + h)),
            pl.BlockSpec(blk, lambda h, m: (m, 2 * N_HEADS + h)),
            pl.BlockSpec((1, ROW_TILE, 2 * ROW_TILE), lambda h, m: (h, 0, 0)),
        ],
        out_specs=pl.BlockSpec(blk, lambda h, m: (m, h)),
        out_shape=jax.ShapeDtypeStruct((N_PROMPT, MIX_ATT), F32),
        compiler_params=_params("parallel", "arbitrary"),
        name="attn_prompt",
    )(p_all, p_all, p_all, p_all, p_all, bias)


def _prompt_bias(table):
    n_lags = 3 * ROW_TILE - 1
    rel = ROW_TILE + (ROW_TILE - 1 - jnp.arange(n_lags))
    per_lag = table[:, jnp.clip(rel, -MAX_REL, MAX_REL) + MAX_REL]
    period = jnp.pad(per_lag, ((0, 0), (0, 1)))
    skew = jnp.tile(period, (1, ROW_TILE))[:, :ROW_TILE * n_lags].reshape(N_HEADS, ROW_TILE, n_lags)
    toeplitz = skew[:, :, ROW_TILE - 1:ROW_TILE - 1 + 2 * ROW_TILE]
    i = np.arange(ROW_TILE)[:, None]
    j = np.arange(2 * ROW_TILE)[None, :]
    chunk_lag = (LEFT_CHUNKS + i // CHUNK) - j // CHUNK
    band = (chunk_lag >= 0) & (chunk_lag <= LEFT_CHUNKS)
    return jnp.where(band[None], toeplitz, NEG_INF).astype(F32)


def _attn_sample_kernel(q_ref, k_ref, v_ref, ck_ref, cv_ref, bc_ref, bn_ref, o_ref):
    tb = (((1,), (1,)), ((), ()))
    for h in range(N_HEADS):
        sl = slice(h * HEAD_DIM, (h + 1) * HEAD_DIM)
        q = q_ref[:, sl].astype(BF16)
        kn = k_ref[:, sl].astype(BF16)
        vn = v_ref[:, sl].astype(BF16)
        kc = ck_ref[0, :, sl].astype(BF16)
        vc = cv_ref[0, :, sl].astype(BF16)
        sc = lax.dot_general(q, kc, tb, preferred_element_type=F32) * SCALE + bc_ref[h]
        sn = lax.dot_general(q, kn, tb, preferred_element_type=F32) * SCALE + bn_ref[h]
        mx = jnp.maximum(jnp.max(sc, axis=-1, keepdims=True), jnp.max(sn, axis=-1, keepdims=True))
        ec = jnp.exp(sc - mx)
        en = jnp.exp(sn - mx)
        tot = jnp.sum(ec, axis=-1, keepdims=True) + jnp.sum(en, axis=-1, keepdims=True)
        o_ref[:, sl] = (jnp.dot((ec / tot).astype(BF16), vc, preferred_element_type=F32)
                        + jnp.dot((en / tot).astype(BF16), vn, preferred_element_type=F32))


def _attn_sample(p_all, cache_k, cache_v, bias_c, bias_n):
    r = cache_k.shape[1]
    first = N_PROMPT // DEC_SEQ
    new = lambda c: pl.BlockSpec((DEC_SEQ, MIX_ATT), lambda b: (first + b, c))
    cache = pl.BlockSpec((1, r, MIX_ATT), lambda b: (b, 0, 0))
    return pl.pallas_call(
        _attn_sample_kernel,
        grid=(DEC_BATCH,),
        in_specs=[new(0), new(1), new(2), cache, cache,
                  pl.BlockSpec((N_HEADS, DEC_SEQ, r), lambda b: (0, 0, 0)),
                  pl.BlockSpec((N_HEADS, DEC_SEQ, DEC_SEQ), lambda b: (0, 0, 0))],
        out_specs=pl.BlockSpec((DEC_SEQ, MIX_ATT), lambda b: (b, 0)),
        out_shape=jax.ShapeDtypeStruct((N_SAMPLE, MIX_ATT), F32),
        compiler_params=_params("parallel"),
        name="attn_sample",
    )(p_all, p_all, p_all, cache_k, cache_v, bias_c, bias_n)


def _sample_bias(table, r):
    rel = (r + jnp.arange(DEC_SEQ))[:, None] - jnp.arange(r + DEC_SEQ)[None, :]
    bias = table[:, jnp.clip(rel, -MAX_REL, MAX_REL) + MAX_REL].astype(F32)
    return bias[:, :, :r], bias[:, :, r:]


def _mix_kernel(ap_ref, as_ref, bg_ref, cg_ref, u_ref, cgh_ref, uh_ref, s1_ref, s2_ref,
                cw_ref, ga_ref, gc_ref, o_ref, z_ref):
    i = pl.program_id(0)
    sample = i >= MIX_PROMPT_TILES
    z = cg_ref[...] * u_ref[...]
    z_ref[...] = z
    zh = jnp.where(i > 0, cgh_ref[...] * uh_ref[...], 0.0)
    row = lax.broadcasted_iota(jnp.int32, z.shape, 0)
    pos = jnp.where(sample, row & (DEC_SEQ - 1), row)
    z1 = jnp.where(pos == 0, jnp.where(sample, s1_ref[...], zh[7:8]), pltpu.roll(z, 1, 0))
    z2 = jnp.where(pos < 2,
                   jnp.where(sample, s2_ref[...], jnp.where(row == 0, zh[6:7], zh[7:8])),
                   pltpu.roll(z, 2, 0))
    y = cw_ref[0:1] * z2
    y = y + cw_ref[1:2] * z1
    y = y + cw_ref[2:3] * z
    att = jnp.where(sample, as_ref[...], ap_ref[...])
    o_ref[:, :MIX_ATT] = _rms(att, ga_ref[...]).astype(BF16)
    o_ref[:, MIX_ATT:] = _rms(bg_ref[...] * y, gc_ref[...]).astype(BF16)


def _mix(att_p, att_s, p_all, s1, s2, conv_w, g_att, g_conv):
    tile = lambda c: pl.BlockSpec((MIX_TILE, MIX_CONV), lambda i: (i, c))
    halo = lambda c: pl.BlockSpec(
        (HALO_ROWS, MIX_CONV), lambda i: (jnp.maximum(i * (MIX_TILE // HALO_ROWS) - 1, 0), c))
    const = lambda shape: pl.BlockSpec(shape, lambda i: (0, 0))
    sample_tile = lambda: pl.BlockSpec((MIX_TILE, MIX_CONV), lambda i: (jnp.maximum(i - MIX_PROMPT_TILES, 0), 0))
    return pl.pallas_call(
        _mix_kernel,
        grid=(N_TOK // MIX_TILE,),
        in_specs=[
            pl.BlockSpec((MIX_TILE, MIX_ATT), lambda i: (jnp.minimum(i, MIX_PROMPT_TILES - 1), 0)),
            sample_tile(),
            tile(3), tile(4), tile(5), halo(4), halo(5),
            sample_tile(), sample_tile(),
            const((3, MIX_CONV)), const((1, MIX_ATT)), const((1, MIX_CONV)),
        ],
        out_specs=[pl.BlockSpec((MIX_TILE, D_MODEL), lambda i: (i, 0)),
                   pl.BlockSpec((MIX_TILE, MIX_CONV), lambda i: (i, 0))],
        out_shape=[jax.ShapeDtypeStruct((N_TOK, D_MODEL), BF16),
                   jax.ShapeDtypeStruct((N_TOK, MIX_CONV), F32)],
        compiler_params=_params("parallel"),
        name="conv_norms",
    )(att_p, att_s, p_all, p_all, p_all, p_all, p_all, s1, s2, conv_w, g_att, g_conv)


def _matmul_residual_kernel(a_ref, w_ref, r_ref, o_ref):
    o_ref[...] = r_ref[...] + jnp.dot(a_ref[...], w_ref[...], preferred_element_type=F32)


def _matmul_residual(a, w, res, tn):
    m, k = a.shape
    n = w.shape[1]
    return pl.pallas_call(
        _matmul_residual_kernel,
        grid=(m // ROW_TILE, n // tn),
        in_specs=[
            pl.BlockSpec((ROW_TILE, k), lambda i, j: (i, 0)),
            pl.BlockSpec((k, tn), lambda i, j: (0, j)),
            pl.BlockSpec((ROW_TILE, tn), lambda i, j: (i, j)),
        ],
        out_specs=pl.BlockSpec((ROW_TILE, tn), lambda i, j: (i, j)),
        out_shape=jax.ShapeDtypeStruct((m, n), F32),
        compiler_params=_params("parallel", "arbitrary"),
        name="out_proj",
    )(a, w, res)


def _router_kernel(y_ref, g_ref, w_ref, b_ref, h_ref, idx_ref, gate_ref):
    h = _rms(y_ref[...], g_ref[...])
    h_ref[...] = h
    logits = jnp.dot(h.astype(BF16), w_ref[...], preferred_element_type=F32) + b_ref[...]
    lane = lax.broadcasted_iota(jnp.int32, logits.shape, 1).astype(F32)
    left = jnp.where(lane < N_EXPERTS, logits, NEG_INF)
    vals, idxs = [], []
    for _ in range(TOP_K):
        top = jnp.max(left, axis=-1, keepdims=True)
        pick = jnp.min(jnp.where(left == top, lane, float(LANES)), axis=-1, keepdims=True)
        vals.append(top)
        idxs.append(pick)
        left = jnp.where(lane == pick, NEG_INF, left)
    exps = [jnp.exp(v - vals[0]) for v in vals]
    tot = exps[0]
    for e in exps[1:]:
        tot = tot + e
    idx_out = jnp.zeros_like(logits)
    gate_out = jnp.zeros_like(logits)
    for k in range(TOP_K):
        idx_out = jnp.where(lane == k, idxs[k], idx_out)
        gate_out = jnp.where(lane == k, exps[k] / tot, gate_out)
    idx_ref[...] = idx_out.astype(jnp.int32)
    gate_ref[...] = gate_out


def _router(y, g, w, b):
    const = lambda shape: pl.BlockSpec(shape, lambda i: (0, 0))
    return pl.pallas_call(
        _router_kernel,
        grid=(N_ROW_TILES,),
        in_specs=[pl.BlockSpec((ROW_TILE, D_MODEL), lambda i: (i, 0)),
                  const((1, D_MODEL)), const((D_MODEL, LANES)), const((1, LANES))],
        out_specs=[pl.BlockSpec((ROW_TILE, D_MODEL), lambda i: (i, 0)),
                   pl.BlockSpec((ROW_TILE, LANES), lambda i: (i, 0)),
                   pl.BlockSpec((ROW_TILE, LANES), lambda i: (i, 0))],
        out_shape=[jax.ShapeDtypeStruct((N_TOK, D_MODEL), F32),
                   jax.ShapeDtypeStruct((N_TOK, LANES), jnp.int32),
                   jax.ShapeDtypeStruct((N_TOK, LANES), F32)],
        compiler_params=_params("parallel"),
        name="ffn_norm_router",
    )(y, g, w, b)


def _routing(top_idx):
    e_flat = top_idx.reshape(-1)
    onehot = (e_flat[:, None] == jnp.arange(N_EXPERTS)[None, :]).astype(jnp.int32)
    csum = jnp.cumsum(onehot, axis=0)
    rank = jnp.take_along_axis(csum, e_flat[:, None], axis=1)[:, 0] - 1
    count = csum[-1]
    n_tiles = (count + MOE_TILE - 1) // MOE_TILE
    tile_end = jnp.cumsum(n_tiles)
    tile_start = tile_end - n_tiles
    pos = tile_start[e_flat] * MOE_TILE + rank
    src = jnp.zeros((MOE_ROWS,), jnp.int32).at[pos].set(jnp.arange(N_PAIRS, dtype=jnp.int32) // TOP_K)
    tile = jnp.arange(MOE_TILES, dtype=jnp.int32)
    owner = jnp.minimum(jnp.sum(tile_end[None, :] <= tile[:, None], axis=1), N_EXPERTS - 1)
    tile_rows = jnp.clip(count[owner] - (tile - tile_start[owner]) * MOE_TILE, 0, MOE_TILE)
    i32 = lambda a: a.astype(jnp.int32)
    return i32(pos), src, i32(n_tiles), i32(tile_start), i32(tile_rows)


def _row_copy(src_hbm, row, buf, slot, sem):
    return pltpu.make_async_copy(src_hbm.at[pl.ds(row, 1), :], buf.at[pl.ds(slot, 1), :], sem)


def _gather_kernel(rows_ref, src_ref, h_hbm, o_ref, buf, sem):
    i = pl.program_id(0)

    @pl.when(i == 0)
    def _():
        buf[...] = jnp.zeros_like(buf)

    groups = (rows_ref[i] + DMA_UNROLL - 1) // DMA_UNROLL

    def start(g, c):
        for u in range(DMA_UNROLL):
            r = g * DMA_UNROLL + u
            _row_copy(h_hbm, src_ref[0, 0, r], buf, r, sem).start()
        return c

    def wait(g, c):
        for u in range(DMA_UNROLL):
            _row_copy(h_hbm, 0, buf, g * DMA_UNROLL + u, sem).wait()
        return c

    lax.fori_loop(0, groups, start, 0)
    lax.fori_loop(0, groups, wait, 0)
    o_ref[...] = buf[...].astype(BF16)


def _gather(tile_rows, src, h):
    return pl.pallas_call(
        _gather_kernel,
        grid_spec=pltpu.PrefetchScalarGridSpec(
            num_scalar_prefetch=1,
            grid=(MOE_TILES,),
            in_specs=[pl.BlockSpec((1, 1, MOE_TILE), lambda i, rows: (i, 0, 0), memory_space=pltpu.SMEM),
                      pl.BlockSpec(memory_space=pl.ANY)],
            out_specs=pl.BlockSpec((MOE_TILE, D_MODEL), lambda i, rows: (i, 0)),
            scratch_shapes=[pltpu.VMEM((MOE_TILE, D_MODEL), F32), pltpu.SemaphoreType.DMA(())],
        ),
        out_shape=jax.ShapeDtypeStruct((MOE_ROWS, D_MODEL), BF16),
        compiler_params=_params("arbitrary"),
        name="gather_rows",
    )(tile_rows, src.reshape(MOE_TILES, 1, MOE_TILE), h)


def _swiglu(gate, up):
    gate = jnp.minimum(gate, SWIGLU_LIMIT)
    up = jnp.clip(up, -SWIGLU_LIMIT, SWIGLU_LIMIT)
    return (up + 1.0) * (gate * jax.nn.sigmoid(SWIGLU_ALPHA * gate))


def _expert_matmul_kernel(nt_ref, ts_ref, x_hbm, w_hbm, *refs, parts, part_stride, epilogue):
    b_refs = refs[:parts]
    o_hbm, xbuf, obuf, wbf, stage, xsem, osem, ssem = refs[parts:]
    tm, tn, ch = xbuf.shape[1], wbf.shape[3], stage.shape[2]
    n_chunks = wbf.shape[2] // ch
    n_col, n_exp = pl.num_programs(0), pl.num_programs(1)
    j, e = pl.program_id(0), pl.program_id(1)
    step = j * n_exp + e
    cur = lax.rem(step, 2)
    has_next = step + 1 < n_col * n_exp
    nxt_step = jnp.minimum(step + 1, n_col * n_exp - 1)
    j_nxt, e_nxt = nxt_step // n_exp, lax.rem(nxt_step, n_exp)
    n_rows, first = nt_ref[e], ts_ref[e]

    def chunk_copies(jj, ee, c):
        slot = lax.rem(c, STAGE_SLOTS)
        rows = pl.ds(pl.multiple_of(c * ch, ch), ch)
        return [pltpu.make_async_copy(
            w_hbm.at[ee, rows, pl.ds(pl.multiple_of(jj * tn + p * part_stride, tn), tn)],
            stage.at[slot, p], ssem.at[slot]) for p in range(parts)]

    def x_copy(t):
        slot = lax.rem(t, 2)
        return pltpu.make_async_copy(
            x_hbm.at[pl.ds(pl.multiple_of((first + t) * tm, tm), tm), :], xbuf.at[slot], xsem.at[slot])

    def tile_store(tile, slot):
        return pltpu.make_async_copy(
            obuf.at[slot],
            o_hbm.at[pl.ds(pl.multiple_of(tile * tm, tm), tm), pl.ds(pl.multiple_of(j * tn, tn), tn)],
            osem.at[slot])

    def o_copy(t):
        return tile_store(first + t, lax.rem(t, 2))

    def start_weights(jj, ee):
        for c in range(STAGE_SLOTS - 1):
            for cp in chunk_copies(jj, ee, c):
                cp.start()

    def chunk_arrive(jj, ee, c):
        @pl.when(c + STAGE_SLOTS - 1 < n_chunks)
        def _():
            for cp in chunk_copies(jj, ee, c + STAGE_SLOTS - 1):
                cp.start()

        for cp in chunk_copies(jj, ee, c):
            cp.wait()

    def chunk_convert(c, dst):
        slot = lax.rem(c, STAGE_SLOTS)
        rows = pl.ds(pl.multiple_of(c * ch, ch), ch)
        for p in range(parts):
            wbf[dst, p, rows, :] = stage[slot, p].astype(BF16)

    def rows_arrive(t):
        @pl.when(t + 1 < n_rows)
        def _():
            x_copy(t + 1).start()

        @pl.when(t >= 2)
        def _():
            o_copy(t - 2).wait()

        x_copy(t).wait()

    def rows_compute(t):
        slot = lax.rem(t, 2)
        x = xbuf[slot]
        acc = [jnp.dot(x, wbf[cur, p], preferred_element_type=F32) + b_refs[p][...] for p in range(parts)]
        obuf[slot] = epilogue(*acc).astype(obuf.dtype)
        o_copy(t).start()

    @pl.when(step == 0)
    def _():
        start_weights(0, 0)

        def load(c, carry):
            chunk_arrive(0, 0, c)
            chunk_convert(c, 0)
            return carry

        lax.fori_loop(0, n_chunks, load, 0)

    @pl.when(has_next)
    def _():
        start_weights(j_nxt, e_nxt)

    @pl.when(n_rows > 0)
    def _():
        x_copy(0).start()

    todo = jnp.where(has_next, n_chunks, 0)
    both = jnp.minimum(n_rows, todo)

    def rows_and_chunk(t, carry):
        rows_arrive(t)
        chunk_arrive(j_nxt, e_nxt, t)
        chunk_convert(t, 1 - cur)
        rows_compute(t)
        return carry

    def rows_only(t, carry):
        rows_arrive(t)
        rows_compute(t)
        return carry

    def chunk_only(c, carry):
        chunk_arrive(j_nxt, e_nxt, c)
        chunk_convert(c, 1 - cur)
        return carry

    lax.fori_loop(0, both, rows_and_chunk, 0)
    lax.fori_loop(both, n_rows, rows_only, 0)
    lax.fori_loop(both, todo, chunk_only, 0)

    @pl.when(n_rows >= 2)
    def _():
        o_copy(n_rows - 2).wait()

    @pl.when(n_rows >= 1)
    def _():
        o_copy(n_rows - 1).wait()

    @pl.when(e == n_exp - 1)
    def _():
        used = first + n_rows
        total = o_hbm.shape[0] // tm
        obuf[0] = jnp.zeros(obuf.shape[1:], obuf.dtype)

        def start(tile, carry):
            tile_store(tile, 0).start()
            return carry

        def wait(tile, carry):
            tile_store(tile, 0).wait()
            return carry

        lax.fori_loop(used, total, start, 0)
        lax.fori_loop(used, total, wait, 0)


def _expert_matmul(n_tiles, tile_start, x, w, b, *, parts, epilogue, out_dtype, name):
    n_exp, k, cols = w.shape
    n = cols // parts
    n_col = n // EXPERT_COLS
    bias = lambda p: pl.BlockSpec((None, 1, EXPERT_COLS), lambda j, e, nt, ts: (e, 0, p * n_col + j))
    any_space = pl.BlockSpec(memory_space=pl.ANY)
    return pl.pallas_call(
        functools.partial(_expert_matmul_kernel, parts=parts, part_stride=n, epilogue=epilogue),
        grid_spec=pltpu.PrefetchScalarGridSpec(
            num_scalar_prefetch=2,
            grid=(n_col, n_exp),
            in_specs=[any_space, any_space] + [bias(p) for p in range(parts)],
            out_specs=any_space,
            scratch_shapes=[
                pltpu.VMEM((2, MOE_TILE, k), BF16),
                pltpu.VMEM((2, MOE_TILE, EXPERT_COLS), out_dtype),
                pltpu.VMEM((2, parts, k, EXPERT_COLS), BF16),
                pltpu.VMEM((STAGE_SLOTS, parts, WEIGHT_CHUNK, EXPERT_COLS), F32),
                pltpu.SemaphoreType.DMA((2,)),
                pltpu.SemaphoreType.DMA((2,)),
                pltpu.SemaphoreType.DMA((STAGE_SLOTS,)),
            ],
        ),
        out_shape=jax.ShapeDtypeStruct((x.shape[0], n), out_dtype),
        compiler_params=_params("arbitrary", "arbitrary"),
        name=name,
    )(n_tiles, tile_start, x, w, *([b] * parts))


def _combine_kernel(pos_ref, y_ref, gate_ref, g_ref, ys_hbm, o_ref, buf, sem):
    per_iter = DMA_UNROLL // TOP_K

    def start(it, c):
        for u in range(per_iter):
            t = it * per_iter + u
            for k in range(TOP_K):
                _row_copy(ys_hbm, pos_ref[0, 0, t * TOP_K + k], buf, k * COMBINE_TILE + t, sem).start()
        return c

    def wait(it, c):
        for u in range(DMA_UNROLL):
            _row_copy(ys_hbm, 0, buf, it * DMA_UNROLL + u, sem).wait()
        return c

    lax.fori_loop(0, COMBINE_TILE // per_iter, start, 0)
    lax.fori_loop(0, TOP_K * COMBINE_TILE // DMA_UNROLL, wait, 0)
    gates = gate_ref[...]
    moe = gates[:, 0:1] * buf[0:COMBINE_TILE, :]
    for k in range(1, TOP_K):
        moe = moe + gates[:, k:k + 1] * buf[k * COMBINE_TILE:(k + 1) * COMBINE_TILE, :]
    o_ref[...] = _rms(y_ref[...] + moe, g_ref[...])


def _combine(pos, y, gates, g, ys, first_row, n_rows, name):
    off = first_row // COMBINE_TILE
    tok = lambda width: pl.BlockSpec((COMBINE_TILE, width), lambda i: (off + i, 0))
    return pl.pallas_call(
        _combine_kernel,
        grid=(n_rows // COMBINE_TILE,),
        in_specs=[pl.BlockSpec((1, 1, TOP_K * COMBINE_TILE), lambda i: (off + i, 0, 0), memory_space=pltpu.SMEM),
                  tok(D_MODEL), tok(LANES),
                  pl.BlockSpec((1, D_MODEL), lambda i: (0, 0)),
                  pl.BlockSpec(memory_space=pl.ANY)],
        out_specs=pl.BlockSpec((COMBINE_TILE, D_MODEL), lambda i: (i, 0)),
        out_shape=jax.ShapeDtypeStruct((n_rows, D_MODEL), F32),
        scratch_shapes=[pltpu.VMEM((TOP_K * COMBINE_TILE, D_MODEL), F32), pltpu.SemaphoreType.DMA(())],
        compiler_params=_params("arbitrary"),
        name=name,
    )(pos.reshape(N_TOK // COMBINE_TILE, 1, TOP_K * COMBINE_TILE), y, gates, g, ys)


def kernel(x_prompt, x_sample, cache_attn_k, cache_attn_v, state_conv, norm_attn, w_in, conv_w,
           rel_table, norm_att_out, norm_conv_out, w_out, norm_ffn, w_router, b_router,
           w_gate_up, b_gate_up, w_down, b_down, norm_final):
    row = lambda g: g.reshape(1, -1)
    x = jnp.concatenate([x_prompt.reshape(N_PROMPT, D_MODEL), x_sample.reshape(N_SAMPLE, D_MODEL)], axis=0)

    p_all = _norm_matmul(x, row(norm_attn[0]), w_in[0].astype(BF16), 1024)

    r = cache_attn_k.shape[2]
    att_p = _attn_prompt(p_all, _prompt_bias(rel_table[0]))
    bias_c, bias_n = _sample_bias(rel_table[0], r)
    att_s = _attn_sample(p_all, cache_attn_k.reshape(DEC_BATCH, r, MIX_ATT),
                         cache_attn_v.reshape(DEC_BATCH, r, MIX_ATT), bias_c, bias_n)

    st = state_conv[0]
    pad = lambda a: jnp.pad(a, ((0, 0), (0, DEC_SEQ - a.shape[1]), (0, 0))).reshape(N_SAMPLE, MIX_CONV)
    merged, z = _mix(att_p, att_s, p_all, pad(st[:, 1:2]), pad(st), conv_w[0],
                     row(norm_att_out[0]), row(norm_conv_out[0]))
    y1 = _matmul_residual(merged, w_out[0].astype(BF16), x, 1024)

    w_r = jnp.pad(w_router[0], ((0, 0), (0, LANES - N_EXPERTS))).astype(BF16)
    b_r = jnp.pad(b_router[0], (0, LANES - N_EXPERTS)).reshape(1, LANES)
    h, idx, gates = _router(y1, row(norm_ffn[0]), w_r, b_r)

    pos, src, n_tiles, tile_start, tile_rows = _routing(idx[:, :TOP_K])
    xs = _gather(tile_rows, src, h)
    hs = _expert_matmul(n_tiles, tile_start, xs, w_gate_up.reshape(N_EXPERTS, D_MODEL, 2 * D_FF),
                        b_gate_up.reshape(N_EXPERTS, 1, 2 * D_FF),
                        parts=2, epilogue=_swiglu, out_dtype=BF16, name="expert_gate_up")
    ys = _expert_matmul(n_tiles, tile_start, hs, w_down.reshape(N_EXPERTS, D_FF, D_MODEL),
                        b_down.reshape(N_EXPERTS, 1, D_MODEL),
                        parts=1, epilogue=lambda a: a, out_dtype=F32, name="expert_down")
    y_p = _combine(pos, y1, gates, row(norm_final), ys, 0, N_PROMPT, "combine_final_norm_prompt")
    y_s = _combine(pos, y1, gates, row(norm_final), ys, N_PROMPT, N_SAMPLE, "combine_final_norm_sample")

    k_all = p_all[:, MIX_ATT:2 * MIX_ATT]
    v_all = p_all[:, 2 * MIX_ATT:3 * MIX_ATT]
    keep = min(BAND_ROWS, SEQ)
    shape_p = (1, 1, keep, N_HEADS, HEAD_DIM)
    shape_s = (1, DEC_BATCH, DEC_SEQ, N_HEADS, HEAD_DIM)
    z_s = z[N_PROMPT:].reshape(DEC_BATCH, DEC_SEQ, MIX_CONV)
    return (y_p.reshape(1, SEQ, D_MODEL),
            y_s.reshape(DEC_BATCH, DEC_SEQ, D_MODEL),
            k_all[N_PROMPT - keep:N_PROMPT].reshape(shape_p),
            v_all[N_PROMPT - keep:N_PROMPT].reshape(shape_p),
            z[N_PROMPT - 2:N_PROMPT].reshape(1, 1, 2, MIX_CONV),
            k_all[N_PROMPT:].reshape(shape_s),
            v_all[N_PROMPT:].reshape(shape_s),
            z_s[:, DEC_SEQ - 2:].reshape(1, DEC_BATCH, 2, MIX_CONV))
```

```python
import functools

import jax
import jax.numpy as jnp
import numpy as np
from jax import lax
from jax.experimental import pallas as pl
from jax.experimental.pallas import tpu as pltpu

D_MODEL = 4096
SEQ = 16384
DEC_BATCH = 32
DEC_SEQ = 16
CHUNK = 64
LEFT_CHUNKS = 8
BAND_ROWS = LEFT_CHUNKS * CHUNK
MIX_ATT = 2048
MIX_CONV = 2048
HEAD_DIM = 128
N_HEADS = 16
MAX_REL = 4 * CHUNK
IN_COLS = 3 * MIX_ATT + 3 * MIX_CONV
N_EXPERTS = 32
TOP_K = 4
D_FF = D_MODEL
SWIGLU_LIMIT = 7.0
SWIGLU_ALPHA = 1.702
EPS = 1e-5
SCALE = HEAD_DIM ** -0.5

N_PROMPT = SEQ
N_SAMPLE = DEC_BATCH * DEC_SEQ
N_TOK = N_PROMPT + N_SAMPLE
N_PAIRS = N_TOK * TOP_K

LANES = 128
ROW_TILE = 512
N_ROW_TILES = N_TOK // ROW_TILE
N_PROMPT_TILES = N_PROMPT // ROW_TILE
HALO_ROWS = 8
MIX_TILE = 128
MIX_PROMPT_TILES = N_PROMPT // MIX_TILE

MOE_TILE = 256
MOE_ROWS = N_PAIRS + N_EXPERTS * MOE_TILE
MOE_TILES = MOE_ROWS // MOE_TILE
EXPERT_COLS = 1024
WEIGHT_CHUNK = 512
STAGE_SLOTS = 3
DMA_UNROLL = 8
COMBINE_TILE = 128

VMEM_LIMIT = 56 * 1024 * 1024

BF16 = jnp.bfloat16
F32 = jnp.float32
NEG_INF = float("-inf")


def _params(*sem):
    return pltpu.CompilerParams(dimension_semantics=sem, vmem_limit_bytes=VMEM_LIMIT)


def _rms(x, g):
    return x * lax.rsqrt(jnp.mean(x * x, axis=-1, keepdims=True) + EPS) * g


def _norm_matmul_kernel(x_ref, g_ref, w_ref, o_ref, h_ref):
    @pl.when(pl.program_id(1) == 0)
    def _():
        h_ref[...] = _rms(x_ref[...], g_ref[...]).astype(BF16)

    o_ref[...] = jnp.dot(h_ref[...], w_ref[...], preferred_element_type=F32)


def _norm_matmul(x, g, w, tn, name):
    m, k = x.shape
    n = w.shape[1]
    return pl.pallas_call(
        _norm_matmul_kernel,
        grid=(m // ROW_TILE, n // tn),
        in_specs=[
            pl.BlockSpec((ROW_TILE, k), lambda i, j: (i, 0)),
            pl.BlockSpec((1, k), lambda i, j: (0, 0)),
            pl.BlockSpec((k, tn), lambda i, j: (0, j)),
        ],
        out_specs=pl.BlockSpec((ROW_TILE, tn), lambda i, j: (i, j)),
        out_shape=jax.ShapeDtypeStruct((m, n), F32),
        scratch_shapes=[pltpu.VMEM((ROW_TILE, k), BF16)],
        compiler_params=_params("parallel", "arbitrary"),
        name=name,
    )(x, g, w)


def _attn_prompt_kernel(q_ref, kp_ref, kc_ref, vp_ref, vc_ref, b_ref, o_ref):
    m = pl.program_id(1)
    q = q_ref[...].astype(BF16)
    k = jnp.concatenate([kp_ref[...], kc_ref[...]], axis=0).astype(BF16)
    v = jnp.concatenate([vp_ref[...], vc_ref[...]], axis=0).astype(BF16)
    s = lax.dot_general(q, k, (((1,), (1,)), ((), ())), preferred_element_type=F32)
    s = s * SCALE + b_ref[0]
    col = lax.broadcasted_iota(jnp.int32, s.shape, 1)
    s = jnp.where(col < jnp.where(m == 0, ROW_TILE, 0), NEG_INF, s)
    e = jnp.exp(s - jnp.max(s, axis=-1, keepdims=True))
    p = (e / jnp.sum(e, axis=-1, keepdims=True)).astype(BF16)
    o_ref[...] = jnp.dot(p, v, preferred_element_type=F32)


def _attn_prompt(p_all, bias):
    blk = (ROW_TILE, HEAD_DIM)
    prev = lambda h, m: jnp.maximum(m - 1, 0)
    return pl.pallas_call(
        _attn_prompt_kernel,
        grid=(N_HEADS, N_PROMPT_TILES),
        in_specs=[
            pl.BlockSpec(blk, lambda h, m: (m, h)),
            pl.BlockSpec(blk, lambda h, m: (prev(h, m), N_HEADS + h)),
            pl.BlockSpec(blk, lambda h, m: (m, N_HEADS + h)),
            pl.BlockSpec(blk, lambda h, m: (prev(h, m), 2 * N_HEADS + h)),
            pl.BlockSpec(blk, lambda h, m: (m, 2 * N_HEADS + h)),
            pl.BlockSpec((1, ROW_TILE, 2 * ROW_TILE), lambda h, m: (h, 0, 0)),
        ],
        out_specs=pl.BlockSpec(blk, lambda h, m: (m, h)),
        out_shape=jax.ShapeDtypeStruct((N_PROMPT, MIX_ATT), F32),
        compiler_params=_params("parallel", "arbitrary"),
        name="attn_prompt",
    )(p_all, p_all, p_all, p_all, p_all, bias)


def _prompt_bias(table):
    n_lags = 3 * ROW_TILE - 1
    rel = ROW_TILE + (ROW_TILE - 1 - jnp.arange(n_lags))
    per_lag = table[:, jnp.clip(rel, -MAX_REL, MAX_REL) + MAX_REL]
    period = jnp.pad(per_lag, ((0, 0), (0, 1)))
    skew = jnp.tile(period, (1, ROW_TILE))[:, :ROW_TILE * n_lags].reshape(N_HEADS, ROW_TILE, n_lags)
    toeplitz = skew[:, :, ROW_TILE - 1:ROW_TILE - 1 + 2 * ROW_TILE]
    i = np.arange(ROW_TILE)[:, None]
    j = np.arange(2 * ROW_TILE)[None, :]
    chunk_lag = (LEFT_CHUNKS + i // CHUNK) - j // CHUNK
    band = (chunk_lag >= 0) & (chunk_lag <= LEFT_CHUNKS)
    return jnp.where(band[None], toeplitz, NEG_INF).astype(F32)


def _attn_sample_kernel(q_ref, k_ref, v_ref, ck_ref, cv_ref, bc_ref, bn_ref, o_ref):
    tb = (((1,), (1,)), ((), ()))
    for h in range(N_HEADS):
        sl = slice(h * HEAD_DIM, (h + 1) * HEAD_DIM)
        q = q_ref[:, sl].astype(BF16)
        kn = k_ref[:, sl].astype(BF16)
        vn = v_ref[:, sl].astype(BF16)
        kc = ck_ref[0, :, sl].astype(BF16)
        vc = cv_ref[0, :, sl].astype(BF16)
        sc = lax.dot_general(q, kc, tb, preferred_element_type=F32) * SCALE + bc_ref[h]
        sn = lax.dot_general(q, kn, tb, preferred_element_type=F32) * SCALE + bn_ref[h]
        mx = jnp.maximum(jnp.max(sc, axis=-1, keepdims=True), jnp.max(sn, axis=-1, keepdims=True))
        ec = jnp.exp(sc - mx)
        en = jnp.exp(sn - mx)
        tot = jnp.sum(ec, axis=-1, keepdims=True) + jnp.sum(en, axis=-1, keepdims=True)
        o_ref[:, sl] = (jnp.dot((ec / tot).astype(BF16), vc, preferred_element_type=F32)
                        + jnp.dot((en / tot).astype(BF16), vn, preferred_element_type=F32))


def _attn_sample(p_s, cache_k, cache_v, bias_c, bias_n):
    r = cache_k.shape[1]
    new = lambda c: pl.BlockSpec((DEC_SEQ, MIX_ATT), lambda b: (b, c))
    cache = pl.BlockSpec((1, r, MIX_ATT), lambda b: (b, 0, 0))
    return pl.pallas_call(
        _attn_sample_kernel,
        grid=(DEC_BATCH,),
        in_specs=[new(0), new(1), new(2), cache, cache,
                  pl.BlockSpec((N_HEADS, DEC_SEQ, r), lambda b: (0, 0, 0)),
                  pl.BlockSpec((N_HEADS, DEC_SEQ, DEC_SEQ), lambda b: (0, 0, 0))],
        out_specs=pl.BlockSpec((DEC_SEQ, MIX_ATT), lambda b: (b, 0)),
        out_shape=jax.ShapeDtypeStruct((N_SAMPLE, MIX_ATT), F32),
        compiler_params=_params("parallel"),
        name="attn_sample",
    )(p_s, p_s, p_s, cache_k, cache_v, bias_c, bias_n)


def _sample_bias(table, r):
    rel = (r + jnp.arange(DEC_SEQ))[:, None] - jnp.arange(r + DEC_SEQ)[None, :]
    bias = table[:, jnp.clip(rel, -MAX_REL, MAX_REL) + MAX_REL].astype(F32)
    return bias[:, :, :r], bias[:, :, r:]


def _mix_kernel(ap_ref, bgp_ref, cgp_ref, up_ref, cgh_ref, uh_ref, as_ref, bgs_ref, cgs_ref, us_ref,
                s1_ref, s2_ref, cw_ref, ga_ref, gc_ref, o_ref, z_ref):
    i = pl.program_id(0)
    sample = i >= MIX_PROMPT_TILES
    z = jnp.where(sample, cgs_ref[...] * us_ref[...], cgp_ref[...] * up_ref[...])
    z_ref[...] = z
    zh = jnp.where(i > 0, cgh_ref[...] * uh_ref[...], 0.0)
    row = lax.broadcasted_iota(jnp.int32, z.shape, 0)
    pos = jnp.where(sample, row & (DEC_SEQ - 1), row)
    z1 = jnp.where(pos == 0, jnp.where(sample, s1_ref[...], zh[7:8]), pltpu.roll(z, 1, 0))
    z2 = jnp.where(pos < 2,
                   jnp.where(sample, s2_ref[...], jnp.where(row == 0, zh[6:7], zh[7:8])),
                   pltpu.roll(z, 2, 0))
    y = cw_ref[0:1] * z2
    y = y + cw_ref[1:2] * z1
    y = y + cw_ref[2:3] * z
    att = jnp.where(sample, as_ref[...], ap_ref[...])
    bg = jnp.where(sample, bgs_ref[...], bgp_ref[...])
    o_ref[:, :MIX_ATT] = _rms(att, ga_ref[...]).astype(BF16)
    o_ref[:, MIX_ATT:] = _rms(bg * y, gc_ref[...]).astype(BF16)


def _mix(att_p, p_p, att_s, p_s, s1, s2, conv_w, g_att, g_conv):
    prompt = lambda c: pl.BlockSpec((MIX_TILE, MIX_CONV), lambda i: (jnp.minimum(i, MIX_PROMPT_TILES - 1), c))
    sample = lambda c: pl.BlockSpec((MIX_TILE, MIX_CONV), lambda i: (jnp.maximum(i - MIX_PROMPT_TILES, 0), c))
    halo = lambda c: pl.BlockSpec(
        (HALO_ROWS, MIX_CONV),
        lambda i: (jnp.clip(i * (MIX_TILE // HALO_ROWS) - 1, 0, N_PROMPT // HALO_ROWS - 1), c))
    const = lambda shape: pl.BlockSpec(shape, lambda i: (0, 0))
    return pl.pallas_call(
        _mix_kernel,
        grid=(N_TOK // MIX_TILE,),
        in_specs=[
            prompt(0), prompt(3), prompt(4), prompt(5), halo(4), halo(5),
            sample(0), sample(3), sample(4), sample(5),
            sample(0), sample(0),
            const((3, MIX_CONV)), const((1, MIX_ATT)), const((1, MIX_CONV)),
        ],
        out_specs=[pl.BlockSpec((MIX_TILE, D_MODEL), lambda i: (i, 0)),
                   pl.BlockSpec((MIX_TILE, MIX_CONV), lambda i: (i, 0))],
        out_shape=[jax.ShapeDtypeStruct((N_TOK, D_MODEL), BF16),
                   jax.ShapeDtypeStruct((N_TOK, MIX_CONV), F32)],
        compiler_params=_params("parallel"),
        name="conv_norms",
    )(att_p, p_p, p_p, p_p, p_p, p_p, att_s, p_s, p_s, p_s, s1, s2, conv_w, g_att, g_conv)


def _matmul_residual_kernel(a_ref, w_ref, rp_ref, rs_ref, o_ref):
    res = jnp.where(pl.program_id(0) < N_PROMPT_TILES, rp_ref[...], rs_ref[...])
    o_ref[...] = res + jnp.dot(a_ref[...], w_ref[...], preferred_element_type=F32)


def _matmul_residual(a, w, res_p, res_s, tn):
    m, k = a.shape
    n = w.shape[1]
    return pl.pallas_call(
        _matmul_residual_kernel,
        grid=(m // ROW_TILE, n // tn),
        in_specs=[
            pl.BlockSpec((ROW_TILE, k), lambda i, j: (i, 0)),
            pl.BlockSpec((k, tn), lambda i, j: (0, j)),
            pl.BlockSpec((ROW_TILE, tn), lambda i, j: (jnp.minimum(i, N_PROMPT_TILES - 1), j)),
            pl.BlockSpec((ROW_TILE, tn), lambda i, j: (0, j)),
        ],
        out_specs=pl.BlockSpec((ROW_TILE, tn), lambda i, j: (i, j)),
        out_shape=jax.ShapeDtypeStruct((m, n), F32),
        compiler_params=_params("parallel", "arbitrary"),
        name="out_proj",
    )(a, w, res_p, res_s)


def _router_kernel(y_ref, g_ref, w_ref, b_ref, h_ref, idx_ref, gate_ref):
    h = _rms(y_ref[...], g_ref[...])
    h_ref[...] = h
    logits = jnp.dot(h.astype(BF16), w_ref[...], preferred_element_type=F32) + b_ref[...]
    lane = lax.broadcasted_iota(jnp.int32, logits.shape, 1).astype(F32)
    left = jnp.where(lane < N_EXPERTS, logits, NEG_INF)
    vals, idxs = [], []
    for _ in range(TOP_K):
        top = jnp.max(left, axis=-1, keepdims=True)
        pick = jnp.min(jnp.where(left == top, lane, float(LANES)), axis=-1, keepdims=True)
        vals.append(top)
        idxs.append(pick)
        left = jnp.where(lane == pick, NEG_INF, left)
    exps = [jnp.exp(v - vals[0]) for v in vals]
    tot = exps[0]
    for e in exps[1:]:
        tot = tot + e
    idx_out = jnp.zeros_like(logits)
    gate_out = jnp.zeros_like(logits)
    for k in range(TOP_K):
        idx_out = jnp.where(lane == k, idxs[k], idx_out)
        gate_out = jnp.where(lane == k, exps[k] / tot, gate_out)
    idx_ref[...] = idx_out.astype(jnp.int32)
    gate_ref[...] = gate_out


def _router(y, g, w, b):
    const = lambda shape: pl.BlockSpec(shape, lambda i: (0, 0))
    return pl.pallas_call(
        _router_kernel,
        grid=(N_ROW_TILES,),
        in_specs=[pl.BlockSpec((ROW_TILE, D_MODEL), lambda i: (i, 0)),
                  const((1, D_MODEL)), const((D_MODEL, LANES)), const((1, LANES))],
        out_specs=[pl.BlockSpec((ROW_TILE, D_MODEL), lambda i: (i, 0)),
                   pl.BlockSpec((ROW_TILE, LANES), lambda i: (i, 0)),
                   pl.BlockSpec((ROW_TILE, LANES), lambda i: (i, 0))],
        out_shape=[jax.ShapeDtypeStruct((N_TOK, D_MODEL), F32),
                   jax.ShapeDtypeStruct((N_TOK, LANES), jnp.int32),
                   jax.ShapeDtypeStruct((N_TOK, LANES), F32)],
        compiler_params=_params("parallel"),
        name="ffn_norm_router",
    )(y, g, w, b)


def _routing(top_idx):
    e_flat = top_idx.reshape(-1)
    onehot = (e_flat[:, None] == jnp.arange(N_EXPERTS)[None, :]).astype(jnp.int32)
    csum = jnp.cumsum(onehot, axis=0)
    rank = jnp.take_along_axis(csum, e_flat[:, None], axis=1)[:, 0] - 1
    count = csum[-1]
    n_tiles = (count + MOE_TILE - 1) // MOE_TILE
    tile_end = jnp.cumsum(n_tiles)
    tile_start = tile_end - n_tiles
    pos = tile_start[e_flat] * MOE_TILE + rank
    src = jnp.zeros((MOE_ROWS,), jnp.int32).at[pos].set(jnp.arange(N_PAIRS, dtype=jnp.int32) // TOP_K)
    tile = jnp.arange(MOE_TILES, dtype=jnp.int32)
    owner = jnp.minimum(jnp.sum(tile_end[None, :] <= tile[:, None], axis=1), N_EXPERTS - 1)
    tile_rows = jnp.clip(count[owner] - (tile - tile_start[owner]) * MOE_TILE, 0, MOE_TILE)
    i32 = lambda a: a.astype(jnp.int32)
    return i32(pos), src, i32(n_tiles), i32(tile_start), i32(tile_rows)


def _row_copy(src_hbm, row, buf, slot, sem):
    return pltpu.make_async_copy(src_hbm.at[pl.ds(row, 1), :], buf.at[pl.ds(slot, 1), :], sem)


def _gather_kernel(rows_ref, src_ref, h_hbm, o_ref, buf, sem):
    i = pl.program_id(0)

    @pl.when(i == 0)
    def _():
        buf[...] = jnp.zeros_like(buf)

    groups = (rows_ref[i] + DMA_UNROLL - 1) // DMA_UNROLL

    def start(g, c):
        for u in range(DMA_UNROLL):
            r = g * DMA_UNROLL + u
            _row_copy(h_hbm, src_ref[0, 0, r], buf, r, sem).start()
        return c

    def wait(g, c):
        for u in range(DMA_UNROLL):
            _row_copy(h_hbm, 0, buf, g * DMA_UNROLL + u, sem).wait()
        return c

    lax.fori_loop(0, groups, start, 0)
    lax.fori_loop(0, groups, wait, 0)
    o_ref[...] = buf[...].astype(BF16)


def _gather(tile_rows, src, h):
    return pl.pallas_call(
        _gather_kernel,
        grid_spec=pltpu.PrefetchScalarGridSpec(
            num_scalar_prefetch=1,
            grid=(MOE_TILES,),
            in_specs=[pl.BlockSpec((1, 1, MOE_TILE), lambda i, rows: (i, 0, 0), memory_space=pltpu.SMEM),
                      pl.BlockSpec(memory_space=pl.ANY)],
            out_specs=pl.BlockSpec((MOE_TILE, D_MODEL), lambda i, rows: (i, 0)),
            scratch_shapes=[pltpu.VMEM((MOE_TILE, D_MODEL), F32), pltpu.SemaphoreType.DMA(())],
        ),
        out_shape=jax.ShapeDtypeStruct((MOE_ROWS, D_MODEL), BF16),
        compiler_params=_params("arbitrary"),
        name="gather_rows",
    )(tile_rows, src.reshape(MOE_TILES, 1, MOE_TILE), h)


def _swiglu(gate, up):
    gate = jnp.minimum(gate, SWIGLU_LIMIT)
    up = jnp.clip(up, -SWIGLU_LIMIT, SWIGLU_LIMIT)
    return (up + 1.0) * (gate * jax.nn.sigmoid(SWIGLU_ALPHA * gate))


def _expert_matmul_kernel(nt_ref, ts_ref, x_hbm, w_hbm, *refs, parts, part_stride, epilogue):
    b_refs = refs[:parts]
    o_hbm, xbuf, obuf, wbf, stage, xsem, osem, ssem = refs[parts:]
    tm, tn, ch = xbuf.shape[1], wbf.shape[3], stage.shape[2]
    n_chunks = wbf.shape[2] // ch
    n_col, n_exp = pl.num_programs(0), pl.num_programs(1)
    j, e = pl.program_id(0), pl.program_id(1)
    step = j * n_exp + e
    cur = lax.rem(step, 2)
    has_next = step + 1 < n_col * n_exp
    nxt_step = jnp.minimum(step + 1, n_col * n_exp - 1)
    j_nxt, e_nxt = nxt_step // n_exp, lax.rem(nxt_step, n_exp)
    n_rows, first = nt_ref[e], ts_ref[e]

    def chunk_copies(jj, ee, c):
        slot = lax.rem(c, STAGE_SLOTS)
        rows = pl.ds(pl.multiple_of(c * ch, ch), ch)
        return [pltpu.make_async_copy(
            w_hbm.at[ee, rows, pl.ds(pl.multiple_of(jj * tn + p * part_stride, tn), tn)],
            stage.at[slot, p], ssem.at[slot]) for p in range(parts)]

    def tile_load(tile, slot):
        return pltpu.make_async_copy(
            x_hbm.at[pl.ds(pl.multiple_of(tile * tm, tm), tm), :], xbuf.at[slot], xsem.at[slot])

    def x_copy(t):
        return tile_load(first + t, lax.rem(t, 2))

    def tile_store(tile, slot):
        return pltpu.make_async_copy(
            obuf.at[slot],
            o_hbm.at[pl.ds(pl.multiple_of(tile * tm, tm), tm), pl.ds(pl.multiple_of(j * tn, tn), tn)],
            osem.at[slot])

    def o_copy(t):
        return tile_store(first + t, lax.rem(t, 2))

    def start_weights(jj, ee):
        for c in range(STAGE_SLOTS - 1):
            for cp in chunk_copies(jj, ee, c):
                cp.start()

    def chunk_arrive(jj, ee, c):
        @pl.when(c + STAGE_SLOTS - 1 < n_chunks)
        def _():
            for cp in chunk_copies(jj, ee, c + STAGE_SLOTS - 1):
                cp.start()

        for cp in chunk_copies(jj, ee, c):
            cp.wait()

    def chunk_convert(c, dst):
        slot = lax.rem(c, STAGE_SLOTS)
        rows = pl.ds(pl.multiple_of(c * ch, ch), ch)
        for p in range(parts):
            wbf[dst, p, rows, :] = stage[slot, p].astype(BF16)

    def rows_arrive(t):
        @pl.when(t + 1 < n_rows)
        def _():
            x_copy(t + 1).start()

        @pl.when(t >= 2)
        def _():
            o_copy(t - 2).wait()

        x_copy(t).wait()

    def rows_compute(t):
        slot = lax.rem(t, 2)
        x = xbuf[slot]
        acc = [jnp.dot(x, wbf[cur, p], preferred_element_type=F32) + b_refs[p][...] for p in range(parts)]
        obuf[slot] = epilogue(*acc).astype(obuf.dtype)
        o_copy(t).start()

    @pl.when(step == 0)
    def _():
        start_weights(0, 0)

        def load(c, carry):
            chunk_arrive(0, 0, c)
            chunk_convert(c, 0)
            return carry

        lax.fori_loop(0, n_chunks, load, 0)

        @pl.when(has_next)
        def _():
            start_weights(j_nxt, e_nxt)

        @pl.when(n_rows > 0)
        def _():
            x_copy(0).start()

    todo = jnp.where(has_next, n_chunks, 0)
    both = jnp.minimum(n_rows, todo)

    def rows_and_chunk(t, carry):
        rows_arrive(t)
        chunk_arrive(j_nxt, e_nxt, t)
        chunk_convert(t, 1 - cur)
        rows_compute(t)
        return carry

    def rows_only(t, carry):
        rows_arrive(t)
        rows_compute(t)
        return carry

    def chunk_only(c, carry):
        chunk_arrive(j_nxt, e_nxt, c)
        chunk_convert(c, 1 - cur)
        return carry

    lax.fori_loop(0, both, rows_and_chunk, 0)
    lax.fori_loop(both, n_rows, rows_only, 0)
    lax.fori_loop(both, todo, chunk_only, 0)

    @pl.when(n_rows >= 2)
    def _():
        o_copy(n_rows - 2).wait()

    @pl.when(n_rows >= 1)
    def _():
        o_copy(n_rows - 1).wait()

    @pl.when(e == n_exp - 1)
    def _():
        used = first + n_rows
        total = o_hbm.shape[0] // tm
        obuf[0] = jnp.zeros(obuf.shape[1:], obuf.dtype)

        def start(tile, carry):
            tile_store(tile, 0).start()
            return carry

        def wait(tile, carry):
            tile_store(tile, 0).wait()
            return carry

        lax.fori_loop(used, total, start, 0)
        lax.fori_loop(used, total, wait, 0)

    after = step + 2

    @pl.when(after < n_col * n_exp)
    def _():
        start_weights(after // n_exp, lax.rem(after, n_exp))

    @pl.when(has_next & (nt_ref[e_nxt] > 0))
    def _():
        tile_load(ts_ref[e_nxt], 0).start()


def _expert_matmul(n_tiles, tile_start, x, w, b, *, parts, epilogue, out_dtype, name):
    n_exp, k, cols = w.shape
    n = cols // parts
    n_col = n // EXPERT_COLS
    bias = lambda p: pl.BlockSpec((None, 1, EXPERT_COLS), lambda j, e, nt, ts: (e, 0, p * n_col + j))
    any_space = pl.BlockSpec(memory_space=pl.ANY)
    return pl.pallas_call(
        functools.partial(_expert_matmul_kernel, parts=parts, part_stride=n, epilogue=epilogue),
        grid_spec=pltpu.PrefetchScalarGridSpec(
            num_scalar_prefetch=2,
            grid=(n_col, n_exp),
            in_specs=[any_space, any_space] + [bias(p) for p in range(parts)],
            out_specs=any_space,
            scratch_shapes=[
                pltpu.VMEM((2, MOE_TILE, k), BF16),
                pltpu.VMEM((2, MOE_TILE, EXPERT_COLS), out_dtype),
                pltpu.VMEM((2, parts, k, EXPERT_COLS), BF16),
                pltpu.VMEM((STAGE_SLOTS, parts, WEIGHT_CHUNK, EXPERT_COLS), F32),
                pltpu.SemaphoreType.DMA((2,)),
                pltpu.SemaphoreType.DMA((2,)),
                pltpu.SemaphoreType.DMA((STAGE_SLOTS,)),
            ],
        ),
        out_shape=jax.ShapeDtypeStruct((x.shape[0], n), out_dtype),
        compiler_params=_params("arbitrary", "arbitrary"),
        name=name,
    )(n_tiles, tile_start, x, w, *([b] * parts))


def _combine_kernel(pos_ref, y_ref, gate_ref, g_ref, ys_hbm, o_ref, buf, sem):
    per_iter = DMA_UNROLL // TOP_K

    def start(it, c):
        for u in range(per_iter):
            t = it * per_iter + u
            for k in range(TOP_K):
                _row_copy(ys_hbm, pos_ref[0, 0, t * TOP_K + k], buf, k * COMBINE_TILE + t, sem).start()
        return c

    def wait(it, c):
        for u in range(DMA_UNROLL):
            _row_copy(ys_hbm, 0, buf, it * DMA_UNROLL + u, sem).wait()
        return c

    lax.fori_loop(0, COMBINE_TILE // per_iter, start, 0)
    lax.fori_loop(0, TOP_K * COMBINE_TILE // DMA_UNROLL, wait, 0)
    gates = gate_ref[...]
    moe = gates[:, 0:1] * buf[0:COMBINE_TILE, :]
    for k in range(1, TOP_K):
        moe = moe + gates[:, k:k + 1] * buf[k * COMBINE_TILE:(k + 1) * COMBINE_TILE, :]
    o_ref[...] = _rms(y_ref[...] + moe, g_ref[...])


def _combine(pos, y, gates, g, ys, first_row, n_rows, name):
    off = first_row // COMBINE_TILE
    tok = lambda width: pl.BlockSpec((COMBINE_TILE, width), lambda i: (off + i, 0))
    return pl.pallas_call(
        _combine_kernel,
        grid=(n_rows // COMBINE_TILE,),
        in_specs=[pl.BlockSpec((1, 1, TOP_K * COMBINE_TILE), lambda i: (off + i, 0, 0), memory_space=pltpu.SMEM),
                  tok(D_MODEL), tok(LANES),
                  pl.BlockSpec((1, D_MODEL), lambda i: (0, 0)),
                  pl.BlockSpec(memory_space=pl.ANY)],
        out_specs=pl.BlockSpec((COMBINE_TILE, D_MODEL), lambda i: (i, 0)),
        out_shape=jax.ShapeDtypeStruct((n_rows, D_MODEL), F32),
        scratch_shapes=[pltpu.VMEM((TOP_K * COMBINE_TILE, D_MODEL), F32), pltpu.SemaphoreType.DMA(())],
        compiler_params=_params("arbitrary"),
        name=name,
    )(pos.reshape(N_TOK // COMBINE_TILE, 1, TOP_K * COMBINE_TILE), y, gates, g, ys)


def kernel(x_prompt, x_sample, cache_attn_k, cache_attn_v, state_conv, norm_attn, w_in, conv_w,
           rel_table, norm_att_out, norm_conv_out, w_out, norm_ffn, w_router, b_router,
           w_gate_up, b_gate_up, w_down, b_down, norm_final):
    row = lambda g: g.reshape(1, -1)
    x_p = x_prompt.reshape(N_PROMPT, D_MODEL)
    x_s = x_sample.reshape(N_SAMPLE, D_MODEL)

    w_in_bf = w_in[0].astype(BF16)
    p_p = _norm_matmul(x_p, row(norm_attn[0]), w_in_bf, 1024, "norm_in_proj_prompt")
    p_s = _norm_matmul(x_s, row(norm_attn[0]), w_in_bf, 1024, "norm_in_proj_sample")

    r = cache_attn_k.shape[2]
    att_p = _attn_prompt(p_p, _prompt_bias(rel_table[0]))
    bias_c, bias_n = _sample_bias(rel_table[0], r)
    att_s = _attn_sample(p_s, cache_attn_k.reshape(DEC_BATCH, r, MIX_ATT),
                         cache_attn_v.reshape(DEC_BATCH, r, MIX_ATT), bias_c, bias_n)

    st = state_conv[0]
    pad = lambda a: jnp.pad(a, ((0, 0), (0, DEC_SEQ - a.shape[1]), (0, 0))).reshape(N_SAMPLE, MIX_CONV)
    merged, z = _mix(att_p, p_p, att_s, p_s, pad(st[:, 1:2]), pad(st), conv_w[0],
                     row(norm_att_out[0]), row(norm_conv_out[0]))
    y1 = _matmul_residual(merged, w_out[0].astype(BF16), x_p, x_s, 1024)

    w_r = jnp.pad(w_router[0], ((0, 0), (0, LANES - N_EXPERTS))).astype(BF16)
    b_r = jnp.pad(b_router[0], (0, LANES - N_EXPERTS)).reshape(1, LANES)
    h, idx, gates = _router(y1, row(norm_ffn[0]), w_r, b_r)

    pos, src, n_tiles, tile_start, tile_rows = _routing(idx[:, :TOP_K])
    xs = _gather(tile_rows, src, h)
    hs = _expert_matmul(n_tiles, tile_start, xs, w_gate_up.reshape(N_EXPERTS, D_MODEL, 2 * D_FF),
                        b_gate_up.reshape(N_EXPERTS, 1, 2 * D_FF),
                        parts=2, epilogue=_swiglu, out_dtype=BF16, name="expert_gate_up")
    ys = _expert_matmul(n_tiles, tile_start, hs, w_down.reshape(N_EXPERTS, D_FF, D_MODEL),
                        b_down.reshape(N_EXPERTS, 1, D_MODEL),
                        parts=1, epilogue=lambda a: a, out_dtype=F32, name="expert_down")
    y_p = _combine(pos, y1, gates, row(norm_final), ys, 0, N_PROMPT, "combine_final_norm_prompt")
    y_s = _combine(pos, y1, gates, row(norm_final), ys, N_PROMPT, N_SAMPLE, "combine_final_norm_sample")

    keep = min(BAND_ROWS, SEQ)
    shape_p = (1, 1, keep, N_HEADS, HEAD_DIM)
    shape_s = (1, DEC_BATCH, DEC_SEQ, N_HEADS, HEAD_DIM)
    z_s = z[N_PROMPT:].reshape(DEC_BATCH, DEC_SEQ, MIX_CONV)
    return (y_p.reshape(1, SEQ, D_MODEL),
            y_s.reshape(DEC_BATCH, DEC_SEQ, D_MODEL),
            p_p[N_PROMPT - keep:, MIX_ATT:2 * MIX_ATT].reshape(shape_p),
            p_p[N_PROMPT - keep:, 2 * MIX_ATT:3 * MIX_ATT].reshape(shape_p),
            z[N_PROMPT - 2:N_PROMPT].reshape(1, 1, 2, MIX_CONV),
            p_s[:, MIX_ATT:2 * MIX_ATT].reshape(shape_s),
            p_s[:, 2 * MIX_ATT:3 * MIX_ATT].reshape(shape_s),
            z_s[:, DEC_SEQ - 2:].reshape(1, DEC_BATCH, 2, MIX_CONV))
```

```python
import functools

import jax
import jax.numpy as jnp
import numpy as np
from jax import lax
from jax.experimental import pallas as pl
from jax.experimental.pallas import tpu as pltpu

D_MODEL = 4096
SEQ = 16384
DEC_BATCH = 32
DEC_SEQ = 16
CHUNK = 64
LEFT_CHUNKS = 8
BAND_ROWS = LEFT_CHUNKS * CHUNK
MIX_ATT = 2048
MIX_CONV = 2048
HEAD_DIM = 128
N_HEADS = 16
MAX_REL = 4 * CHUNK
IN_COLS = 3 * MIX_ATT + 3 * MIX_CONV
N_EXPERTS = 32
TOP_K = 4
D_FF = D_MODEL
SWIGLU_LIMIT = 7.0
SWIGLU_ALPHA = 1.702
EPS = 1e-5
SCALE = HEAD_DIM ** -0.5

N_PROMPT = SEQ
N_SAMPLE = DEC_BATCH * DEC_SEQ
N_TOK = N_PROMPT + N_SAMPLE
N_PAIRS = N_TOK * TOP_K

LANES = 128
ROW_TILE = 512
N_ROW_TILES = N_TOK // ROW_TILE
N_PROMPT_TILES = N_PROMPT // ROW_TILE
Q_ROWS = 2 * CHUNK
BAND_KEYS = BAND_ROWS + Q_ROWS
HALO_ROWS = 8
MIX_TILE = 128
MIX_PROMPT_TILES = N_PROMPT // MIX_TILE

MOE_TILE = 256
MOE_ROWS = N_PAIRS + N_EXPERTS * MOE_TILE
MOE_TILES = MOE_ROWS // MOE_TILE
EXPERT_COLS = 2048
WEIGHT_CHUNK = 512
STAGE_SLOTS = 2
DMA_UNROLL = 8
COMBINE_TILE = 128

VMEM_LIMIT = 56 * 1024 * 1024

BF16 = jnp.bfloat16
F32 = jnp.float32
NEG_INF = float("-inf")


def _params(*sem):
    return pltpu.CompilerParams(dimension_semantics=sem, vmem_limit_bytes=VMEM_LIMIT)


def _rms(x, g):
    return x * lax.rsqrt(jnp.mean(x * x, axis=-1, keepdims=True) + EPS) * g


def _norm_matmul_kernel(x_ref, g_ref, w_ref, o_ref, h_ref):
    @pl.when(pl.program_id(1) == 0)
    def _():
        h_ref[...] = _rms(x_ref[...], g_ref[...]).astype(BF16)

    o_ref[...] = jnp.dot(h_ref[...], w_ref[...], preferred_element_type=F32)


def _norm_matmul(x, g, w, tn, name):
    m, k = x.shape
    n = w.shape[1]
    return pl.pallas_call(
        _norm_matmul_kernel,
        grid=(m // ROW_TILE, n // tn),
        in_specs=[
            pl.BlockSpec((ROW_TILE, k), lambda i, j: (i, 0)),
            pl.BlockSpec((1, k), lambda i, j: (0, 0)),
            pl.BlockSpec((k, tn), lambda i, j: (0, j)),
        ],
        out_specs=pl.BlockSpec((ROW_TILE, tn), lambda i, j: (i, j)),
        out_shape=jax.ShapeDtypeStruct((m, n), F32),
        scratch_shapes=[pltpu.VMEM((ROW_TILE, k), BF16)],
        compiler_params=_params("parallel", "arbitrary"),
        name=name,
    )(x, g, w)


def _attn_prompt_kernel(q_ref, kp_ref, kc_ref, vp_ref, vc_ref, b_ref, o_ref):
    m = pl.program_id(1)
    k = jnp.concatenate([kp_ref[...], kc_ref[...]], axis=0).astype(BF16)
    v = jnp.concatenate([vp_ref[...], vc_ref[...]], axis=0).astype(BF16)
    col = lax.broadcasted_iota(jnp.int32, (Q_ROWS, BAND_KEYS), 1)
    starts = range(0, ROW_TILE, Q_ROWS)
    scores = []
    for lo in starts:
        q = q_ref[lo:lo + Q_ROWS, :].astype(BF16)
        s = lax.dot_general(q, k[lo:lo + BAND_KEYS], (((1,), (1,)), ((), ())), preferred_element_type=F32)
        s = s * SCALE + b_ref[0]
        scores.append(jnp.where(col < jnp.where(m == 0, ROW_TILE - lo, 0), NEG_INF, s))
    probs = []
    for s in scores:
        e = jnp.exp(s - jnp.max(s, axis=-1, keepdims=True))
        probs.append((e / jnp.sum(e, axis=-1, keepdims=True)).astype(BF16))
    for lo, p in zip(starts, probs):
        o_ref[lo:lo + Q_ROWS, :] = jnp.dot(p, v[lo:lo + BAND_KEYS], preferred_element_type=F32)


def _attn_prompt(p_all, bias):
    blk = (ROW_TILE, HEAD_DIM)
    prev = lambda h, m: jnp.maximum(m - 1, 0)
    return pl.pallas_call(
        _attn_prompt_kernel,
        grid=(N_HEADS, N_PROMPT_TILES),
        in_specs=[
            pl.BlockSpec(blk, lambda h, m: (m, h)),
            pl.BlockSpec(blk, lambda h, m: (prev(h, m), N_HEADS + h)),
            pl.BlockSpec(blk, lambda h, m: (m, N_HEADS + h)),
            pl.BlockSpec(blk, lambda h, m: (prev(h, m), 2 * N_HEADS + h)),
            pl.BlockSpec(blk, lambda h, m: (m, 2 * N_HEADS + h)),
            pl.BlockSpec((1, Q_ROWS, BAND_KEYS), lambda h, m: (h, 0, 0)),
        ],
        out_specs=pl.BlockSpec(blk, lambda h, m: (m, h)),
        out_shape=jax.ShapeDtypeStruct((N_PROMPT, MIX_ATT), F32),
        compiler_params=_params("parallel", "arbitrary"),
        name="attn_prompt",
    )(p_all, p_all, p_all, p_all, p_all, bias)


def _prompt_bias(table):
    n_lags = Q_ROWS + BAND_KEYS - 1
    rel = BAND_ROWS + (Q_ROWS - 1 - jnp.arange(n_lags))
    per_lag = table[:, jnp.clip(rel, -MAX_REL, MAX_REL) + MAX_REL]
    period = jnp.pad(per_lag, ((0, 0), (0, 1)))
    skew = jnp.tile(period, (1, Q_ROWS))[:, :Q_ROWS * n_lags].reshape(N_HEADS, Q_ROWS, n_lags)
    toeplitz = skew[:, :, Q_ROWS - 1:Q_ROWS - 1 + BAND_KEYS]
    i = np.arange(Q_ROWS)[:, None]
    j = np.arange(BAND_KEYS)[None, :]
    chunk_lag = (LEFT_CHUNKS + i // CHUNK) - j // CHUNK
    band = (chunk_lag >= 0) & (chunk_lag <= LEFT_CHUNKS)
    return jnp.where(band[None], toeplitz, NEG_INF).astype(F32)


def _attn_sample_kernel(q_ref, k_ref, v_ref, ck_ref, cv_ref, bc_ref, bn_ref, o_ref):
    tb = (((1,), (1,)), ((), ()))
    heads = [slice(h * HEAD_DIM, (h + 1) * HEAD_DIM) for h in range(N_HEADS)]
    scores = []
    for h, sl in enumerate(heads):
        q = q_ref[:, sl].astype(BF16)
        sc = lax.dot_general(q, ck_ref[0, :, sl].astype(BF16), tb, preferred_element_type=F32)
        sn = lax.dot_general(q, k_ref[:, sl].astype(BF16), tb, preferred_element_type=F32)
        scores.append((sc * SCALE + bc_ref[h], sn * SCALE + bn_ref[h]))
    probs = []
    for sc, sn in scores:
        mx = jnp.maximum(jnp.max(sc, axis=-1, keepdims=True), jnp.max(sn, axis=-1, keepdims=True))
        ec = jnp.exp(sc - mx)
        en = jnp.exp(sn - mx)
        tot = jnp.sum(ec, axis=-1, keepdims=True) + jnp.sum(en, axis=-1, keepdims=True)
        probs.append(((ec / tot).astype(BF16), (en / tot).astype(BF16)))
    for sl, (pc, pn) in zip(heads, probs):
        o_ref[:, sl] = (jnp.dot(pc, cv_ref[0, :, sl].astype(BF16), preferred_element_type=F32)
                        + jnp.dot(pn, v_ref[:, sl].astype(BF16), preferred_element_type=F32))


def _attn_sample(p_s, cache_k, cache_v, bias_c, bias_n):
    r = cache_k.shape[1]
    new = lambda c: pl.BlockSpec((DEC_SEQ, MIX_ATT), lambda b: (b, c))
    cache = pl.BlockSpec((1, r, MIX_ATT), lambda b: (b, 0, 0))
    return pl.pallas_call(
        _attn_sample_kernel,
        grid=(DEC_BATCH,),
        in_specs=[new(0), new(1), new(2), cache, cache,
                  pl.BlockSpec((N_HEADS, DEC_SEQ, r), lambda b: (0, 0, 0)),
                  pl.BlockSpec((N_HEADS, DEC_SEQ, DEC_SEQ), lambda b: (0, 0, 0))],
        out_specs=pl.BlockSpec((DEC_SEQ, MIX_ATT), lambda b: (b, 0)),
        out_shape=jax.ShapeDtypeStruct((N_SAMPLE, MIX_ATT), F32),
        compiler_params=_params("parallel"),
        name="attn_sample",
    )(p_s, p_s, p_s, cache_k, cache_v, bias_c, bias_n)


def _sample_bias(table, r):
    rel = (r + jnp.arange(DEC_SEQ))[:, None] - jnp.arange(r + DEC_SEQ)[None, :]
    bias = table[:, jnp.clip(rel, -MAX_REL, MAX_REL) + MAX_REL].astype(F32)
    return bias[:, :, :r], bias[:, :, r:]


def _mix_kernel(ap_ref, bgp_ref, cgp_ref, up_ref, cgh_ref, uh_ref, as_ref, bgs_ref, cgs_ref, us_ref,
                s1_ref, s2_ref, cw_ref, ga_ref, gc_ref, o_ref, z_ref):
    i = pl.program_id(0)
    sample = i >= MIX_PROMPT_TILES
    z = jnp.where(sample, cgs_ref[...] * us_ref[...], cgp_ref[...] * up_ref[...])
    z_ref[...] = z
    zh = jnp.where(i > 0, cgh_ref[...] * uh_ref[...], 0.0)
    row = lax.broadcasted_iota(jnp.int32, z.shape, 0)
    pos = jnp.where(sample, row & (DEC_SEQ - 1), row)
    z1 = jnp.where(pos == 0, jnp.where(sample, s1_ref[...], zh[7:8]), pltpu.roll(z, 1, 0))
    z2 = jnp.where(pos < 2,
                   jnp.where(sample, s2_ref[...], jnp.where(row == 0, zh[6:7], zh[7:8])),
                   pltpu.roll(z, 2, 0))
    y = cw_ref[0:1] * z2
    y = y + cw_ref[1:2] * z1
    y = y + cw_ref[2:3] * z
    att = jnp.where(sample, as_ref[...], ap_ref[...])
    bg = jnp.where(sample, bgs_ref[...], bgp_ref[...])
    o_ref[:, :MIX_ATT] = _rms(att, ga_ref[...]).astype(BF16)
    o_ref[:, MIX_ATT:] = _rms(bg * y, gc_ref[...]).astype(BF16)


def _mix(att_p, p_p, att_s, p_s, s1, s2, conv_w, g_att, g_conv):
    prompt = lambda c: pl.BlockSpec((MIX_TILE, MIX_CONV), lambda i: (jnp.minimum(i, MIX_PROMPT_TILES - 1), c))
    sample = lambda c: pl.BlockSpec((MIX_TILE, MIX_CONV), lambda i: (jnp.maximum(i - MIX_PROMPT_TILES, 0), c))
    halo = lambda c: pl.BlockSpec(
        (HALO_ROWS, MIX_CONV),
        lambda i: (jnp.clip(i * (MIX_TILE // HALO_ROWS) - 1, 0, N_PROMPT // HALO_ROWS - 1), c))
    const = lambda shape: pl.BlockSpec(shape, lambda i: (0, 0))
    return pl.pallas_call(
        _mix_kernel,
        grid=(N_TOK // MIX_TILE,),
        in_specs=[
            prompt(0), prompt(3), prompt(4), prompt(5), halo(4), halo(5),
            sample(0), sample(3), sample(4), sample(5),
            sample(0), sample(0),
            const((3, MIX_CONV)), const((1, MIX_ATT)), const((1, MIX_CONV)),
        ],
        out_specs=[pl.BlockSpec((MIX_TILE, D_MODEL), lambda i: (i, 0)),
                   pl.BlockSpec((MIX_TILE, MIX_CONV), lambda i: (i, 0))],
        out_shape=[jax.ShapeDtypeStruct((N_TOK, D_MODEL), BF16),
                   jax.ShapeDtypeStruct((N_TOK, MIX_CONV), F32)],
        compiler_params=_params("parallel"),
        name="conv_norms",
    )(att_p, p_p, p_p, p_p, p_p, p_p, att_s, p_s, p_s, p_s, s1, s2, conv_w, g_att, g_conv)


def _matmul_residual_kernel(a_ref, w_ref, rp_ref, rs_ref, o_ref):
    res = jnp.where(pl.program_id(0) < N_PROMPT_TILES, rp_ref[...], rs_ref[...])
    o_ref[...] = res + jnp.dot(a_ref[...], w_ref[...], preferred_element_type=F32)


def _matmul_residual(a, w, res_p, res_s, tn):
    m, k = a.shape
    n = w.shape[1]
    return pl.pallas_call(
        _matmul_residual_kernel,
        grid=(m // ROW_TILE, n // tn),
        in_specs=[
            pl.BlockSpec((ROW_TILE, k), lambda i, j: (i, 0)),
            pl.BlockSpec((k, tn), lambda i, j: (0, j)),
            pl.BlockSpec((ROW_TILE, tn), lambda i, j: (jnp.minimum(i, N_PROMPT_TILES - 1), j)),
            pl.BlockSpec((ROW_TILE, tn), lambda i, j: (0, j)),
        ],
        out_specs=pl.BlockSpec((ROW_TILE, tn), lambda i, j: (i, j)),
        out_shape=jax.ShapeDtypeStruct((m, n), F32),
        compiler_params=_params("parallel", "arbitrary"),
        name="out_proj",
    )(a, w, res_p, res_s)


def _router_kernel(y_ref, g_ref, w_ref, b_ref, h_ref, idx_ref, gate_ref):
    h = _rms(y_ref[...], g_ref[...])
    h_ref[...] = h
    logits = jnp.dot(h.astype(BF16), w_ref[...], preferred_element_type=F32) + b_ref[...]
    lane = lax.broadcasted_iota(jnp.int32, logits.shape, 1).astype(F32)
    left = jnp.where(lane < N_EXPERTS, logits, NEG_INF)
    vals, idxs = [], []
    for _ in range(TOP_K):
        top = jnp.max(left, axis=-1, keepdims=True)
        pick = jnp.min(jnp.where(left == top, lane, float(LANES)), axis=-1, keepdims=True)
        vals.append(top)
        idxs.append(pick)
        left = jnp.where(lane == pick, NEG_INF, left)
    exps = [jnp.exp(v - vals[0]) for v in vals]
    tot = exps[0]
    for e in exps[1:]:
        tot = tot + e
    idx_out = jnp.zeros_like(logits)
    gate_out = jnp.zeros_like(logits)
    for k in range(TOP_K):
        idx_out = jnp.where(lane == k, idxs[k], idx_out)
        gate_out = jnp.where(lane == k, exps[k] / tot, gate_out)
    idx_ref[...] = idx_out.astype(jnp.int32)
    gate_ref[...] = gate_out


def _router(y, g, w, b):
    const = lambda shape: pl.BlockSpec(shape, lambda i: (0, 0))
    return pl.pallas_call(
        _router_kernel,
        grid=(N_ROW_TILES,),
        in_specs=[pl.BlockSpec((ROW_TILE, D_MODEL), lambda i: (i, 0)),
                  const((1, D_MODEL)), const((D_MODEL, LANES)), const((1, LANES))],
        out_specs=[pl.BlockSpec((ROW_TILE, D_MODEL), lambda i: (i, 0)),
                   pl.BlockSpec((ROW_TILE, LANES), lambda i: (i, 0)),
                   pl.BlockSpec((ROW_TILE, LANES), lambda i: (i, 0))],
        out_shape=[jax.ShapeDtypeStruct((N_TOK, D_MODEL), F32),
                   jax.ShapeDtypeStruct((N_TOK, LANES), jnp.int32),
                   jax.ShapeDtypeStruct((N_TOK, LANES), F32)],
        compiler_params=_params("parallel"),
        name="ffn_norm_router",
    )(y, g, w, b)


def _routing(top_idx):
    e_flat = top_idx.reshape(-1)
    onehot = (e_flat[:, None] == jnp.arange(N_EXPERTS)[None, :]).astype(jnp.int32)
    csum = jnp.cumsum(onehot, axis=0)
    rank = jnp.take_along_axis(csum, e_flat[:, None], axis=1)[:, 0] - 1
    count = csum[-1]
    n_tiles = (count + MOE_TILE - 1) // MOE_TILE
    tile_end = jnp.cumsum(n_tiles)
    tile_start = tile_end - n_tiles
    pos = tile_start[e_flat] * MOE_TILE + rank
    src = jnp.zeros((MOE_ROWS,), jnp.int32).at[pos].set(jnp.arange(N_PAIRS, dtype=jnp.int32) // TOP_K)
    tile = jnp.arange(MOE_TILES, dtype=jnp.int32)
    owner = jnp.minimum(jnp.sum(tile_end[None, :] <= tile[:, None], axis=1), N_EXPERTS - 1)
    tile_rows = jnp.clip(count[owner] - (tile - tile_start[owner]) * MOE_TILE, 0, MOE_TILE)
    i32 = lambda a: a.astype(jnp.int32)
    return i32(pos), src, i32(n_tiles), i32(tile_start), i32(tile_rows)


def _row_copy(src_hbm, row, buf, slot, sem):
    return pltpu.make_async_copy(src_hbm.at[pl.ds(row, 1), :], buf.at[pl.ds(slot, 1), :], sem)


def _gather_kernel(rows_ref, src_ref, h_hbm, o_ref, buf, sem):
    i = pl.program_id(0)

    @pl.when(i == 0)
    def _():
        buf[...] = jnp.zeros_like(buf)

    groups = (rows_ref[i] + DMA_UNROLL - 1) // DMA_UNROLL

    def start(g, c):
        for u in range(DMA_UNROLL):
            r = g * DMA_UNROLL + u
            _row_copy(h_hbm, src_ref[0, 0, r], buf, r, sem).start()
        return c

    def wait(g, c):
        for u in range(DMA_UNROLL):
            _row_copy(h_hbm, 0, buf, g * DMA_UNROLL + u, sem).wait()
        return c

    lax.fori_loop(0, groups, start, 0)
    lax.fori_loop(0, groups, wait, 0)
    o_ref[...] = buf[...].astype(BF16)


def _gather(tile_rows, src, h):
    return pl.pallas_call(
        _gather_kernel,
        grid_spec=pltpu.PrefetchScalarGridSpec(
            num_scalar_prefetch=1,
            grid=(MOE_TILES,),
            in_specs=[pl.BlockSpec((1, 1, MOE_TILE), lambda i, rows: (i, 0, 0), memory_space=pltpu.SMEM),
                      pl.BlockSpec(memory_space=pl.ANY)],
            out_specs=pl.BlockSpec((MOE_TILE, D_MODEL), lambda i, rows: (i, 0)),
            scratch_shapes=[pltpu.VMEM((MOE_TILE, D_MODEL), F32), pltpu.SemaphoreType.DMA(())],
        ),
        out_shape=jax.ShapeDtypeStruct((MOE_ROWS, D_MODEL), BF16),
        compiler_params=_params("arbitrary"),
        name="gather_rows",
    )(tile_rows, src.reshape(MOE_TILES, 1, MOE_TILE), h)


def _swiglu(gate, up):
    gate = jnp.minimum(gate, SWIGLU_LIMIT)
    up = jnp.clip(up, -SWIGLU_LIMIT, SWIGLU_LIMIT)
    return (up + 1.0) * (gate * jax.nn.sigmoid(SWIGLU_ALPHA * gate))


def _expert_matmul_kernel(nt_ref, ts_ref, x_hbm, w_hbm, *refs, parts, part_stride, epilogue):
    b_refs = refs[:parts]
    o_hbm, xbuf, obuf, wbf, stage, xsem, osem, ssem = refs[parts:]
    tm, tn, ch = xbuf.shape[1], wbf.shape[3], stage.shape[2]
    n_chunks = wbf.shape[2] // ch
    n_col, n_exp = pl.num_programs(0), pl.num_programs(1)
    j, e = pl.program_id(0), pl.program_id(1)
    step = j * n_exp + e
    cur = lax.rem(step, 2)
    has_next = step + 1 < n_col * n_exp
    nxt_step = jnp.minimum(step + 1, n_col * n_exp - 1)
    j_nxt, e_nxt = nxt_step // n_exp, lax.rem(nxt_step, n_exp)
    n_rows, first = nt_ref[e], ts_ref[e]

    def chunk_copies(jj, ee, c):
        slot = lax.rem(c, STAGE_SLOTS)
        rows = pl.ds(pl.multiple_of(c * ch, ch), ch)
        return [pltpu.make_async_copy(
            w_hbm.at[ee, rows, pl.ds(pl.multiple_of(jj * tn + p * part_stride, tn), tn)],
            stage.at[slot, p], ssem.at[slot]) for p in range(parts)]

    def tile_load(tile, slot):
        return pltpu.make_async_copy(
            x_hbm.at[pl.ds(pl.multiple_of(tile * tm, tm), tm), :], xbuf.at[slot], xsem.at[slot])

    def x_copy(t):
        return tile_load(first + t, lax.rem(t, 2))

    def tile_store(tile, slot):
        return pltpu.make_async_copy(
            obuf.at[slot],
            o_hbm.at[pl.ds(pl.multiple_of(tile * tm, tm), tm), pl.ds(pl.multiple_of(j * tn, tn), tn)],
            osem.at[slot])

    def o_copy(t):
        return tile_store(first + t, lax.rem(t, 2))

    def start_weights(jj, ee):
        for c in range(STAGE_SLOTS - 1):
            for cp in chunk_copies(jj, ee, c):
                cp.start()

    def chunk_arrive(jj, ee, c):
        @pl.when(c + STAGE_SLOTS - 1 < n_chunks)
        def _():
            for cp in chunk_copies(jj, ee, c + STAGE_SLOTS - 1):
                cp.start()

        for cp in chunk_copies(jj, ee, c):
            cp.wait()

    def chunk_convert(c, dst):
        slot = lax.rem(c, STAGE_SLOTS)
        rows = pl.ds(pl.multiple_of(c * ch, ch), ch)
        for p in range(parts):
            wbf[dst, p, rows, :] = stage[slot, p].astype(BF16)

    def rows_arrive(t):
        @pl.when(t + 1 < n_rows)
        def _():
            x_copy(t + 1).start()

        @pl.when(t >= 2)
        def _():
            o_copy(t - 2).wait()

        x_copy(t).wait()

    def rows_compute(t):
        slot = lax.rem(t, 2)
        x = xbuf[slot]
        acc = [jnp.dot(x, wbf[cur, p], preferred_element_type=F32) + b_refs[p][...] for p in range(parts)]
        obuf[slot] = epilogue(*acc).astype(obuf.dtype)
        o_copy(t).start()

    @pl.when(step == 0)
    def _():
        start_weights(0, 0)

        def load(c, carry):
            chunk_arrive(0, 0, c)
            chunk_convert(c, 0)
            return carry

        lax.fori_loop(0, n_chunks, load, 0)

        @pl.when(has_next)
        def _():
            start_weights(j_nxt, e_nxt)

        @pl.when(n_rows > 0)
        def _():
            x_copy(0).start()

    todo = jnp.where(has_next, n_chunks, 0)

    def rows_and_chunk(t, carry):
        rows_arrive(t)

        @pl.when(t < todo)
        def _():
            chunk_arrive(j_nxt, e_nxt, t)

        chunk_convert(jnp.minimum(t, n_chunks - 1), 1 - cur)
        rows_compute(t)
        return carry

    def chunk_only(c, carry):
        chunk_arrive(j_nxt, e_nxt, c)
        chunk_convert(c, 1 - cur)
        return carry

    lax.fori_loop(0, n_rows, rows_and_chunk, 0)
    lax.fori_loop(jnp.minimum(n_rows, todo), todo, chunk_only, 0)

    @pl.when(n_rows >= 2)
    def _():
        o_copy(n_rows - 2).wait()

    @pl.when(n_rows >= 1)
    def _():
        o_copy(n_rows - 1).wait()

    @pl.when(e == n_exp - 1)
    def _():
        used = first + n_rows
        total = o_hbm.shape[0] // tm
        obuf[0] = jnp.zeros(obuf.shape[1:], obuf.dtype)

        def start(tile, carry):
            tile_store(tile, 0).start()
            return carry

        def wait(tile, carry):
            tile_store(tile, 0).wait()
            return carry

        lax.fori_loop(used, total, start, 0)
        lax.fori_loop(used, total, wait, 0)

    after = step + 2

    @pl.when(after < n_col * n_exp)
    def _():
        start_weights(after // n_exp, lax.rem(after, n_exp))

    @pl.when(has_next & (nt_ref[e_nxt] > 0))
    def _():
        tile_load(ts_ref[e_nxt], 0).start()


def _expert_matmul(n_tiles, tile_start, x, w, b, *, parts, epilogue, out_dtype, name):
    n_exp, k, cols = w.shape
    n = cols // parts
    tn = EXPERT_COLS // parts
    n_col = n // tn
    bias = lambda p: pl.BlockSpec((None, 1, tn), lambda j, e, nt, ts: (e, 0, p * n_col + j))
    any_space = pl.BlockSpec(memory_space=pl.ANY)
    return pl.pallas_call(
        functools.partial(_expert_matmul_kernel, parts=parts, part_stride=n, epilogue=epilogue),
        grid_spec=pltpu.PrefetchScalarGridSpec(
            num_scalar_prefetch=2,
            grid=(n_col, n_exp),
            in_specs=[any_space, any_space] + [bias(p) for p in range(parts)],
            out_specs=any_space,
            scratch_shapes=[
                pltpu.VMEM((2, MOE_TILE, k), BF16),
                pltpu.VMEM((2, MOE_TILE, tn), out_dtype),
                pltpu.VMEM((2, parts, k, tn), BF16),
                pltpu.VMEM((STAGE_SLOTS, parts, WEIGHT_CHUNK, tn), F32),
                pltpu.SemaphoreType.DMA((2,)),
                pltpu.SemaphoreType.DMA((2,)),
                pltpu.SemaphoreType.DMA((STAGE_SLOTS,)),
            ],
        ),
        out_shape=jax.ShapeDtypeStruct((x.shape[0], n), out_dtype),
        compiler_params=_params("arbitrary", "arbitrary"),
        name=name,
    )(n_tiles, tile_start, x, w, *([b] * parts))


def _combine_kernel(pos_ref, y_ref, gate_ref, g_ref, ys_hbm, o_ref, buf, sem):
    per_iter = DMA_UNROLL // TOP_K

    def start(it, c):
        for u in range(per_iter):
            t = it * per_iter + u
            for k in range(TOP_K):
                _row_copy(ys_hbm, pos_ref[0, 0, t * TOP_K + k], buf, k * COMBINE_TILE + t, sem).start()
        return c

    def wait(it, c):
        for u in range(DMA_UNROLL):
            _row_copy(ys_hbm, 0, buf, it * DMA_UNROLL + u, sem).wait()
        return c

    lax.fori_loop(0, COMBINE_TILE // per_iter, start, 0)
    lax.fori_loop(0, TOP_K * COMBINE_TILE // DMA_UNROLL, wait, 0)
    gates = gate_ref[...]
    moe = gates[:, 0:1] * buf[0:COMBINE_TILE, :]
    for k in range(1, TOP_K):
        moe = moe + gates[:, k:k + 1] * buf[k * COMBINE_TILE:(k + 1) * COMBINE_TILE, :]
    o_ref[...] = _rms(y_ref[...] + moe, g_ref[...])


def _combine(pos, y, gates, g, ys, first_row, n_rows, name):
    off = first_row // COMBINE_TILE
    tok = lambda width: pl.BlockSpec((COMBINE_TILE, width), lambda i: (off + i, 0))
    return pl.pallas_call(
        _combine_kernel,
        grid=(n_rows // COMBINE_TILE,),
        in_specs=[pl.BlockSpec((1, 1, TOP_K * COMBINE_TILE), lambda i: (off + i, 0, 0), memory_space=pltpu.SMEM),
                  tok(D_MODEL), tok(LANES),
                  pl.BlockSpec((1, D_MODEL), lambda i: (0, 0)),
                  pl.BlockSpec(memory_space=pl.ANY)],
        out_specs=pl.BlockSpec((COMBINE_TILE, D_MODEL), lambda i: (i, 0)),
        out_shape=jax.ShapeDtypeStruct((n_rows, D_MODEL), F32),
        scratch_shapes=[pltpu.VMEM((TOP_K * COMBINE_TILE, D_MODEL), F32), pltpu.SemaphoreType.DMA(())],
        compiler_params=_params("arbitrary"),
        name=name,
    )(pos.reshape(N_TOK // COMBINE_TILE, 1, TOP_K * COMBINE_TILE), y, gates, g, ys)


def kernel(x_prompt, x_sample, cache_attn_k, cache_attn_v, state_conv, norm_attn, w_in, conv_w,
           rel_table, norm_att_out, norm_conv_out, w_out, norm_ffn, w_router, b_router,
           w_gate_up, b_gate_up, w_down, b_down, norm_final):
    row = lambda g: g.reshape(1, -1)
    x_p = x_prompt.reshape(N_PROMPT, D_MODEL)
    x_s = x_sample.reshape(N_SAMPLE, D_MODEL)

    w_in_bf = w_in[0].astype(BF16)
    p_p = _norm_matmul(x_p, row(norm_attn[0]), w_in_bf, 1024, "norm_in_proj_prompt")
    p_s = _norm_matmul(x_s, row(norm_attn[0]), w_in_bf, 1024, "norm_in_proj_sample")

    r = cache_attn_k.shape[2]
    att_p = _attn_prompt(p_p, _prompt_bias(rel_table[0]))
    bias_c, bias_n = _sample_bias(rel_table[0], r)
    att_s = _attn_sample(p_s, cache_attn_k.reshape(DEC_BATCH, r, MIX_ATT),
                         cache_attn_v.reshape(DEC_BATCH, r, MIX_ATT), bias_c, bias_n)

    st = state_conv[0]
    pad = lambda a: jnp.pad(a, ((0, 0), (0, DEC_SEQ - a.shape[1]), (0, 0))).reshape(N_SAMPLE, MIX_CONV)
    merged, z = _mix(att_p, p_p, att_s, p_s, pad(st[:, 1:2]), pad(st), conv_w[0],
                     row(norm_att_out[0]), row(norm_conv_out[0]))
    y1 = _matmul_residual(merged, w_out[0].astype(BF16), x_p, x_s, 1024)

    w_r = jnp.pad(w_router[0], ((0, 0), (0, LANES - N_EXPERTS))).astype(BF16)
    b_r = jnp.pad(b_router[0], (0, LANES - N_EXPERTS)).reshape(1, LANES)
    h, idx, gates = _router(y1, row(norm_ffn[0]), w_r, b_r)

    pos, src, n_tiles, tile_start, tile_rows = _routing(idx[:, :TOP_K])
    xs = _gather(tile_rows, src, h)
    hs = _expert_matmul(n_tiles, tile_start, xs, w_gate_up.reshape(N_EXPERTS, D_MODEL, 2 * D_FF),
                        b_gate_up.reshape(N_EXPERTS, 1, 2 * D_FF),
                        parts=2, epilogue=_swiglu, out_dtype=BF16, name="expert_gate_up")
    ys = _expert_matmul(n_tiles, tile_start, hs, w_down.reshape(N_EXPERTS, D_FF, D_MODEL),
                        b_down.reshape(N_EXPERTS, 1, D_MODEL),
                        parts=1, epilogue=lambda a: a, out_dtype=F32, name="expert_down")
    y_p = _combine(pos, y1, gates, row(norm_final), ys, 0, N_PROMPT, "combine_final_norm_prompt")
    y_s = _combine(pos, y1, gates, row(norm_final), ys, N_PROMPT, N_SAMPLE, "combine_final_norm_sample")

    keep = min(BAND_ROWS, SEQ)
    shape_p = (1, 1, keep, N_HEADS, HEAD_DIM)
    shape_s = (1, DEC_BATCH, DEC_SEQ, N_HEADS, HEAD_DIM)
    z_s = z[N_PROMPT:].reshape(DEC_BATCH, DEC_SEQ, MIX_CONV)
    return (y_p.reshape(1, SEQ, D_MODEL),
            y_s.reshape(DEC_BATCH, DEC_SEQ, D_MODEL),
            p_p[N_PROMPT - keep:, MIX_ATT:2 * MIX_ATT].reshape(shape_p),
            p_p[N_PROMPT - keep:, 2 * MIX_ATT:3 * MIX_ATT].reshape(shape_p),
            z[N_PROMPT - 2:N_PROMPT].reshape(1, 1, 2, MIX_CONV),
            p_s[:, MIX_ATT:2 * MIX_ATT].reshape(shape_s),
            p_s[:, 2 * MIX_ATT:3 * MIX_ATT].reshape(shape_s),
            z_s[:, DEC_SEQ - 2:].reshape(1, DEC_BATCH, 2, MIX_CONV))
```

```python
import functools

import jax
import jax.numpy as jnp
import numpy as np
from jax import lax
from jax.experimental import pallas as pl
from jax.experimental.pallas import tpu as pltpu

D_MODEL = 4096
SEQ = 16384
DEC_BATCH = 32
DEC_SEQ = 16
CHUNK = 64
LEFT_CHUNKS = 8
BAND_ROWS = LEFT_CHUNKS * CHUNK
MIX_ATT = 2048
MIX_CONV = 2048
HEAD_DIM = 128
N_HEADS = 16
MAX_REL = 4 * CHUNK
IN_COLS = 3 * MIX_ATT + 3 * MIX_CONV
N_EXPERTS = 32
TOP_K = 4
D_FF = D_MODEL
SWIGLU_LIMIT = 7.0
SWIGLU_ALPHA = 1.702
EPS = 1e-5
SCALE = HEAD_DIM ** -0.5

N_PROMPT = SEQ
N_SAMPLE = DEC_BATCH * DEC_SEQ
N_TOK = N_PROMPT + N_SAMPLE
N_PAIRS = N_TOK * TOP_K

LANES = 128
ROW_TILE = 512
N_ROW_TILES = N_TOK // ROW_TILE
N_PROMPT_TILES = N_PROMPT // ROW_TILE
Q_ROWS = 2 * CHUNK
BAND_KEYS = BAND_ROWS + Q_ROWS
HALO_ROWS = 8
MIX_TILE = 128
MIX_PROMPT_TILES = N_PROMPT // MIX_TILE

MOE_TILE = 272
MOE_TILES = -(-N_PAIRS // MOE_TILE) + N_EXPERTS
MOE_ROWS = MOE_TILES * MOE_TILE
EXPERT_COLS = 2048
WEIGHT_CHUNK = 512
STAGE_SLOTS = 2
DMA_UNROLL = 8
COMBINE_TILE = 128

VMEM_LIMIT = 56 * 1024 * 1024

BF16 = jnp.bfloat16
F32 = jnp.float32
NEG_INF = float("-inf")


def _params(*sem):
    return pltpu.CompilerParams(dimension_semantics=sem, vmem_limit_bytes=VMEM_LIMIT)


def _rms(x, g):
    return x * lax.rsqrt(jnp.mean(x * x, axis=-1, keepdims=True) + EPS) * g


def _norm_matmul_kernel(x_ref, g_ref, w_ref, o_ref, h_ref):
    @pl.when(pl.program_id(1) == 0)
    def _():
        h_ref[...] = _rms(x_ref[...], g_ref[...]).astype(BF16)

    o_ref[...] = jnp.dot(h_ref[...], w_ref[...], preferred_element_type=F32)


def _norm_matmul(x, g, w, tn, name):
    m, k = x.shape
    n = w.shape[1]
    return pl.pallas_call(
        _norm_matmul_kernel,
        grid=(m // ROW_TILE, n // tn),
        in_specs=[
            pl.BlockSpec((ROW_TILE, k), lambda i, j: (i, 0)),
            pl.BlockSpec((1, k), lambda i, j: (0, 0)),
            pl.BlockSpec((k, tn), lambda i, j: (0, j)),
        ],
        out_specs=pl.BlockSpec((ROW_TILE, tn), lambda i, j: (i, j)),
        out_shape=jax.ShapeDtypeStruct((m, n), F32),
        scratch_shapes=[pltpu.VMEM((ROW_TILE, k), BF16)],
        compiler_params=_params("parallel", "arbitrary"),
        name=name,
    )(x, g, w)


def _attn_prompt_kernel(q_ref, kp_ref, kc_ref, vp_ref, vc_ref, b_ref, o_ref):
    m = pl.program_id(1)
    k = jnp.concatenate([kp_ref[...], kc_ref[...]], axis=0).astype(BF16)
    v = jnp.concatenate([vp_ref[...], vc_ref[...]], axis=0).astype(BF16)
    col = lax.broadcasted_iota(jnp.int32, (Q_ROWS, BAND_KEYS), 1)
    starts = range(0, ROW_TILE, Q_ROWS)
    scores = []
    for lo in starts:
        q = q_ref[lo:lo + Q_ROWS, :].astype(BF16)
        s = lax.dot_general(q, k[lo:lo + BAND_KEYS], (((1,), (1,)), ((), ())), preferred_element_type=F32)
        s = s * SCALE + b_ref[0]
        scores.append(jnp.where(col < jnp.where(m == 0, ROW_TILE - lo, 0), NEG_INF, s))
    probs = []
    for s in scores:
        e = jnp.exp(s - jnp.max(s, axis=-1, keepdims=True))
        probs.append((e / jnp.sum(e, axis=-1, keepdims=True)).astype(BF16))
    for lo, p in zip(starts, probs):
        o_ref[lo:lo + Q_ROWS, :] = jnp.dot(p, v[lo:lo + BAND_KEYS], preferred_element_type=F32)


def _attn_prompt(p_all, bias):
    blk = (ROW_TILE, HEAD_DIM)
    prev = lambda h, m: jnp.maximum(m - 1, 0)
    return pl.pallas_call(
        _attn_prompt_kernel,
        grid=(N_HEADS, N_PROMPT_TILES),
        in_specs=[
            pl.BlockSpec(blk, lambda h, m: (m, h)),
            pl.BlockSpec(blk, lambda h, m: (prev(h, m), N_HEADS + h)),
            pl.BlockSpec(blk, lambda h, m: (m, N_HEADS + h)),
            pl.BlockSpec(blk, lambda h, m: (prev(h, m), 2 * N_HEADS + h)),
            pl.BlockSpec(blk, lambda h, m: (m, 2 * N_HEADS + h)),
            pl.BlockSpec((1, Q_ROWS, BAND_KEYS), lambda h, m: (h, 0, 0)),
        ],
        out_specs=pl.BlockSpec(blk, lambda h, m: (m, h)),
        out_shape=jax.ShapeDtypeStruct((N_PROMPT, MIX_ATT), F32),
        compiler_params=_params("parallel", "arbitrary"),
        name="attn_prompt",
    )(p_all, p_all, p_all, p_all, p_all, bias)


def _prompt_bias(table):
    n_lags = Q_ROWS + BAND_KEYS - 1
    rel = BAND_ROWS + (Q_ROWS - 1 - jnp.arange(n_lags))
    per_lag = table[:, jnp.clip(rel, -MAX_REL, MAX_REL) + MAX_REL]
    period = jnp.pad(per_lag, ((0, 0), (0, 1)))
    skew = jnp.tile(period, (1, Q_ROWS))[:, :Q_ROWS * n_lags].reshape(N_HEADS, Q_ROWS, n_lags)
    toeplitz = skew[:, :, Q_ROWS - 1:Q_ROWS - 1 + BAND_KEYS]
    i = np.arange(Q_ROWS)[:, None]
    j = np.arange(BAND_KEYS)[None, :]
    chunk_lag = (LEFT_CHUNKS + i // CHUNK) - j // CHUNK
    band = (chunk_lag >= 0) & (chunk_lag <= LEFT_CHUNKS)
    return jnp.where(band[None], toeplitz, NEG_INF).astype(F32)


def _attn_sample_kernel(q_ref, k_ref, v_ref, ck_ref, cv_ref, bc_ref, bn_ref, o_ref):
    tb = (((1,), (1,)), ((), ()))
    heads = [slice(h * HEAD_DIM, (h + 1) * HEAD_DIM) for h in range(N_HEADS)]
    scores = []
    for h, sl in enumerate(heads):
        q = q_ref[:, sl].astype(BF16)
        sc = lax.dot_general(q, ck_ref[0, :, sl].astype(BF16), tb, preferred_element_type=F32)
        sn = lax.dot_general(q, k_ref[:, sl].astype(BF16), tb, preferred_element_type=F32)
        scores.append((sc * SCALE + bc_ref[h], sn * SCALE + bn_ref[h]))
    probs = []
    for sc, sn in scores:
        mx = jnp.maximum(jnp.max(sc, axis=-1, keepdims=True), jnp.max(sn, axis=-1, keepdims=True))
        ec = jnp.exp(sc - mx)
        en = jnp.exp(sn - mx)
        tot = jnp.sum(ec, axis=-1, keepdims=True) + jnp.sum(en, axis=-1, keepdims=True)
        probs.append(((ec / tot).astype(BF16), (en / tot).astype(BF16)))
    for sl, (pc, pn) in zip(heads, probs):
        o_ref[:, sl] = (jnp.dot(pc, cv_ref[0, :, sl].astype(BF16), preferred_element_type=F32)
                        + jnp.dot(pn, v_ref[:, sl].astype(BF16), preferred_element_type=F32))


def _attn_sample(p_s, cache_k, cache_v, bias_c, bias_n):
    r = cache_k.shape[1]
    new = lambda c: pl.BlockSpec((DEC_SEQ, MIX_ATT), lambda b: (b, c))
    cache = pl.BlockSpec((1, r, MIX_ATT), lambda b: (b, 0, 0))
    return pl.pallas_call(
        _attn_sample_kernel,
        grid=(DEC_BATCH,),
        in_specs=[new(0), new(1), new(2), cache, cache,
                  pl.BlockSpec((N_HEADS, DEC_SEQ, r), lambda b: (0, 0, 0)),
                  pl.BlockSpec((N_HEADS, DEC_SEQ, DEC_SEQ), lambda b: (0, 0, 0))],
        out_specs=pl.BlockSpec((DEC_SEQ, MIX_ATT), lambda b: (b, 0)),
        out_shape=jax.ShapeDtypeStruct((N_SAMPLE, MIX_ATT), F32),
        compiler_params=_params("parallel"),
        name="attn_sample",
    )(p_s, p_s, p_s, cache_k, cache_v, bias_c, bias_n)


def _sample_bias(table, r):
    rel = (r + jnp.arange(DEC_SEQ))[:, None] - jnp.arange(r + DEC_SEQ)[None, :]
    bias = table[:, jnp.clip(rel, -MAX_REL, MAX_REL) + MAX_REL].astype(F32)
    return bias[:, :, :r], bias[:, :, r:]


def _mix_kernel(ap_ref, bgp_ref, cgp_ref, up_ref, cgh_ref, uh_ref, as_ref, bgs_ref, cgs_ref, us_ref,
                s1_ref, s2_ref, cw_ref, ga_ref, gc_ref, o_ref, z_ref):
    i = pl.program_id(0)
    sample = i >= MIX_PROMPT_TILES
    z = jnp.where(sample, cgs_ref[...] * us_ref[...], cgp_ref[...] * up_ref[...])
    z_ref[...] = z
    zh = jnp.where(i > 0, cgh_ref[...] * uh_ref[...], 0.0)
    row = lax.broadcasted_iota(jnp.int32, z.shape, 0)
    pos = jnp.where(sample, row & (DEC_SEQ - 1), row)
    z1 = jnp.where(pos == 0, jnp.where(sample, s1_ref[...], zh[7:8]), pltpu.roll(z, 1, 0))
    z2 = jnp.where(pos < 2,
                   jnp.where(sample, s2_ref[...], jnp.where(row == 0, zh[6:7], zh[7:8])),
                   pltpu.roll(z, 2, 0))
    y = cw_ref[0:1] * z2
    y = y + cw_ref[1:2] * z1
    y = y + cw_ref[2:3] * z
    att = jnp.where(sample, as_ref[...], ap_ref[...])
    bg = jnp.where(sample, bgs_ref[...], bgp_ref[...])
    o_ref[:, :MIX_ATT] = _rms(att, ga_ref[...]).astype(BF16)
    o_ref[:, MIX_ATT:] = _rms(bg * y, gc_ref[...]).astype(BF16)


def _mix(att_p, p_p, att_s, p_s, s1, s2, conv_w, g_att, g_conv):
    prompt = lambda c: pl.BlockSpec((MIX_TILE, MIX_CONV), lambda i: (jnp.minimum(i, MIX_PROMPT_TILES - 1), c))
    sample = lambda c: pl.BlockSpec((MIX_TILE, MIX_CONV), lambda i: (jnp.maximum(i - MIX_PROMPT_TILES, 0), c))
    halo = lambda c: pl.BlockSpec(
        (HALO_ROWS, MIX_CONV),
        lambda i: (jnp.clip(i * (MIX_TILE // HALO_ROWS) - 1, 0, N_PROMPT // HALO_ROWS - 1), c))
    const = lambda shape: pl.BlockSpec(shape, lambda i: (0, 0))
    return pl.pallas_call(
        _mix_kernel,
        grid=(N_TOK // MIX_TILE,),
        in_specs=[
            prompt(0), prompt(3), prompt(4), prompt(5), halo(4), halo(5),
            sample(0), sample(3), sample(4), sample(5),
            sample(0), sample(0),
            const((3, MIX_CONV)), const((1, MIX_ATT)), const((1, MIX_CONV)),
        ],
        out_specs=[pl.BlockSpec((MIX_TILE, D_MODEL), lambda i: (i, 0)),
                   pl.BlockSpec((MIX_TILE, MIX_CONV), lambda i: (i, 0))],
        out_shape=[jax.ShapeDtypeStruct((N_TOK, D_MODEL), BF16),
                   jax.ShapeDtypeStruct((N_TOK, MIX_CONV), F32)],
        compiler_params=_params("parallel"),
        name="conv_norms",
    )(att_p, p_p, p_p, p_p, p_p, p_p, att_s, p_s, p_s, p_s, s1, s2, conv_w, g_att, g_conv)


def _matmul_residual_kernel(a_ref, w_ref, rp_ref, rs_ref, o_ref):
    res = jnp.where(pl.program_id(0) < N_PROMPT_TILES, rp_ref[...], rs_ref[...])
    o_ref[...] = res + jnp.dot(a_ref[...], w_ref[...], preferred_element_type=F32)


def _matmul_residual(a, w, res_p, res_s, tn):
    m, k = a.shape
    n = w.shape[1]
    return pl.pallas_call(
        _matmul_residual_kernel,
        grid=(m // ROW_TILE, n // tn),
        in_specs=[
            pl.BlockSpec((ROW_TILE, k), lambda i, j: (i, 0)),
            pl.BlockSpec((k, tn), lambda i, j: (0, j)),
            pl.BlockSpec((ROW_TILE, tn), lambda i, j: (jnp.minimum(i, N_PROMPT_TILES - 1), j)),
            pl.BlockSpec((ROW_TILE, tn), lambda i, j: (0, j)),
        ],
        out_specs=pl.BlockSpec((ROW_TILE, tn), lambda i, j: (i, j)),
        out_shape=jax.ShapeDtypeStruct((m, n), F32),
        compiler_params=_params("parallel", "arbitrary"),
        name="out_proj",
    )(a, w, res_p, res_s)


def _router_kernel(y_ref, g_ref, w_ref, b_ref, h_ref, idx_ref, gate_ref):
    h = _rms(y_ref[...], g_ref[...])
    h_ref[...] = h
    logits = jnp.dot(h.astype(BF16), w_ref[...], preferred_element_type=F32) + b_ref[...]
    lane = lax.broadcasted_iota(jnp.int32, logits.shape, 1).astype(F32)
    left = jnp.where(lane < N_EXPERTS, logits, NEG_INF)
    vals, idxs = [], []
    for _ in range(TOP_K):
        top = jnp.max(left, axis=-1, keepdims=True)
        pick = jnp.min(jnp.where(left == top, lane, float(LANES)), axis=-1, keepdims=True)
        vals.append(top)
        idxs.append(pick)
        left = jnp.where(lane == pick, NEG_INF, left)
    exps = [jnp.exp(v - vals[0]) for v in vals]
    tot = exps[0]
    for e in exps[1:]:
        tot = tot + e
    idx_out = jnp.zeros_like(logits)
    gate_out = jnp.zeros_like(logits)
    for k in range(TOP_K):
        idx_out = jnp.where(lane == k, idxs[k], idx_out)
        gate_out = jnp.where(lane == k, exps[k] / tot, gate_out)
    idx_ref[...] = idx_out.astype(jnp.int32)
    gate_ref[...] = gate_out


def _router(y, g, w, b):
    const = lambda shape: pl.BlockSpec(shape, lambda i: (0, 0))
    return pl.pallas_call(
        _router_kernel,
        grid=(N_ROW_TILES,),
        in_specs=[pl.BlockSpec((ROW_TILE, D_MODEL), lambda i: (i, 0)),
                  const((1, D_MODEL)), const((D_MODEL, LANES)), const((1, LANES))],
        out_specs=[pl.BlockSpec((ROW_TILE, D_MODEL), lambda i: (i, 0)),
                   pl.BlockSpec((ROW_TILE, LANES), lambda i: (i, 0)),
                   pl.BlockSpec((ROW_TILE, LANES), lambda i: (i, 0))],
        out_shape=[jax.ShapeDtypeStruct((N_TOK, D_MODEL), F32),
                   jax.ShapeDtypeStruct((N_TOK, LANES), jnp.int32),
                   jax.ShapeDtypeStruct((N_TOK, LANES), F32)],
        compiler_params=_params("parallel"),
        name="ffn_norm_router",
    )(y, g, w, b)


def _routing(top_idx):
    e_flat = top_idx.reshape(-1)
    onehot = (e_flat[:, None] == jnp.arange(N_EXPERTS)[None, :]).astype(jnp.int32)
    csum = jnp.cumsum(onehot, axis=0)
    rank = jnp.take_along_axis(csum, e_flat[:, None], axis=1)[:, 0] - 1
    count = csum[-1]
    n_tiles = (count + MOE_TILE - 1) // MOE_TILE
    tile_end = jnp.cumsum(n_tiles)
    tile_start = tile_end - n_tiles
    pos = tile_start[e_flat] * MOE_TILE + rank
    src = jnp.zeros((MOE_ROWS,), jnp.int32).at[pos].set(jnp.arange(N_PAIRS, dtype=jnp.int32) // TOP_K)
    tile = jnp.arange(MOE_TILES, dtype=jnp.int32)
    owner = jnp.minimum(jnp.sum(tile_end[None, :] <= tile[:, None], axis=1), N_EXPERTS - 1)
    tile_rows = jnp.clip(count[owner] - (tile - tile_start[owner]) * MOE_TILE, 0, MOE_TILE)
    i32 = lambda a: a.astype(jnp.int32)
    return i32(pos), src, i32(n_tiles), i32(tile_start), i32(tile_rows)


def _row_copy(src_hbm, row, buf, slot, sem):
    return pltpu.make_async_copy(src_hbm.at[pl.ds(row, 1), :], buf.at[pl.ds(slot, 1), :], sem)


def _gather_kernel(rows_ref, src_ref, src_next_ref, h_hbm, o_ref, buf, sem):
    i = pl.program_id(0)
    slot = lax.rem(i, 2)

    def groups(tile):
        return (rows_ref[tile] + DMA_UNROLL - 1) // DMA_UNROLL

    def request(tile, idx_ref, dst):
        def start(g, c):
            for u in range(DMA_UNROLL):
                r = g * DMA_UNROLL + u
                _row_copy(h_hbm, idx_ref[0, 0, r], buf.at[dst], r, sem.at[dst]).start()
            return c

        lax.fori_loop(0, groups(tile), start, 0)

    @pl.when(i == 0)
    def _():
        buf[...] = jnp.zeros_like(buf)
        request(0, src_ref, 0)

    @pl.when(i + 1 < pl.num_programs(0))
    def _():
        request(i + 1, src_next_ref, 1 - slot)

    def wait(g, c):
        for u in range(DMA_UNROLL):
            _row_copy(h_hbm, 0, buf.at[slot], g * DMA_UNROLL + u, sem.at[slot]).wait()
        return c

    lax.fori_loop(0, groups(i), wait, 0)
    o_ref[...] = buf[slot].astype(BF16)


def _gather(tile_rows, src, h):
    src = src.reshape(MOE_TILES, 1, MOE_TILE)
    idx = lambda step: pl.BlockSpec(
        (1, 1, MOE_TILE), lambda i, rows: (jnp.minimum(i + step, MOE_TILES - 1), 0, 0), memory_space=pltpu.SMEM)
    return pl.pallas_call(
        _gather_kernel,
        grid_spec=pltpu.PrefetchScalarGridSpec(
            num_scalar_prefetch=1,
            grid=(MOE_TILES,),
            in_specs=[idx(0), idx(1), pl.BlockSpec(memory_space=pl.ANY)],
            out_specs=pl.BlockSpec((MOE_TILE, D_MODEL), lambda i, rows: (i, 0)),
            scratch_shapes=[pltpu.VMEM((2, MOE_TILE, D_MODEL), F32), pltpu.SemaphoreType.DMA((2,))],
        ),
        out_shape=jax.ShapeDtypeStruct((MOE_ROWS, D_MODEL), BF16),
        compiler_params=_params("arbitrary"),
        name="gather_rows",
    )(tile_rows, src, src, h)


def _swiglu(gate, up):
    gate = jnp.minimum(gate, SWIGLU_LIMIT)
    up = jnp.clip(up, -SWIGLU_LIMIT, SWIGLU_LIMIT)
    return (up + 1.0) * (gate * jax.nn.sigmoid(SWIGLU_ALPHA * gate))


def _expert_matmul_kernel(nt_ref, ts_ref, x_hbm, w_hbm, *refs, parts, part_stride, epilogue):
    b_refs = refs[:parts]
    o_hbm, xbuf, obuf, wbf, stage, xsem, osem, ssem = refs[parts:]
    tm, tn, ch = xbuf.shape[1], wbf.shape[3], stage.shape[2]
    n_chunks = wbf.shape[2] // ch
    n_col, n_exp = pl.num_programs(0), pl.num_programs(1)
    j, e = pl.program_id(0), pl.program_id(1)
    step = j * n_exp + e
    cur = lax.rem(step, 2)
    has_next = step + 1 < n_col * n_exp
    nxt_step = jnp.minimum(step + 1, n_col * n_exp - 1)
    j_nxt, e_nxt = nxt_step // n_exp, lax.rem(nxt_step, n_exp)
    n_rows, first = nt_ref[e], ts_ref[e]

    def chunk_copies(jj, ee, c):
        slot = lax.rem(c, STAGE_SLOTS)
        rows = pl.ds(pl.multiple_of(c * ch, ch), ch)
        return [pltpu.make_async_copy(
            w_hbm.at[ee, rows, pl.ds(pl.multiple_of(jj * tn + p * part_stride, tn), tn)],
            stage.at[slot, p], ssem.at[slot]) for p in range(parts)]

    def tile_load(tile, slot):
        return pltpu.make_async_copy(
            x_hbm.at[pl.ds(pl.multiple_of(tile * tm, tm), tm), :], xbuf.at[slot], xsem.at[slot])

    def x_copy(t):
        return tile_load(first + t, lax.rem(t, 2))

    def tile_store(tile, slot):
        return pltpu.make_async_copy(
            obuf.at[slot],
            o_hbm.at[pl.ds(pl.multiple_of(tile * tm, tm), tm), pl.ds(pl.multiple_of(j * tn, tn), tn)],
            osem.at[slot])

    def o_copy(t):
        return tile_store(first + t, lax.rem(t, 2))

    def start_weights(jj, ee):
        for c in range(STAGE_SLOTS - 1):
            for cp in chunk_copies(jj, ee, c):
                cp.start()

    def chunk_arrive(jj, ee, c):
        @pl.when(c + STAGE_SLOTS - 1 < n_chunks)
        def _():
            for cp in chunk_copies(jj, ee, c + STAGE_SLOTS - 1):
                cp.start()

        for cp in chunk_copies(jj, ee, c):
            cp.wait()

    def chunk_convert(c, dst):
        slot = lax.rem(c, STAGE_SLOTS)
        rows = pl.ds(pl.multiple_of(c * ch, ch), ch)
        for p in range(parts):
            wbf[dst, p, rows, :] = stage[slot, p].astype(BF16)

    def rows_arrive(t):
        @pl.when(t + 1 < n_rows)
        def _():
            x_copy(t + 1).start()

        @pl.when(t >= 2)
        def _():
            o_copy(t - 2).wait()

        x_copy(t).wait()

    def rows_compute(t):
        slot = lax.rem(t, 2)
        x = xbuf[slot]
        acc = [jnp.dot(x, wbf[cur, p], preferred_element_type=F32) + b_refs[p][...] for p in range(parts)]
        obuf[slot] = epilogue(*acc).astype(obuf.dtype)
        o_copy(t).start()

    @pl.when(step == 0)
    def _():
        start_weights(0, 0)

        def load(c, carry):
            chunk_arrive(0, 0, c)
            chunk_convert(c, 0)
            return carry

        lax.fori_loop(0, n_chunks, load, 0)

        @pl.when(has_next)
        def _():
            start_weights(j_nxt, e_nxt)

        @pl.when(n_rows > 0)
        def _():
            x_copy(0).start()

    todo = jnp.where(has_next, n_chunks, 0)

    def rows_and_chunk(t, carry):
        rows_arrive(t)

        @pl.when(t < todo)
        def _():
            chunk_arrive(j_nxt, e_nxt, t)

        chunk_convert(jnp.minimum(t, n_chunks - 1), 1 - cur)
        rows_compute(t)
        return carry

    def chunk_only(c, carry):
        chunk_arrive(j_nxt, e_nxt, c)
        chunk_convert(c, 1 - cur)
        return carry

    lax.fori_loop(0, n_rows, rows_and_chunk, 0)
    lax.fori_loop(jnp.minimum(n_rows, todo), todo, chunk_only, 0)

    @pl.when(n_rows >= 2)
    def _():
        o_copy(n_rows - 2).wait()

    @pl.when(n_rows >= 1)
    def _():
        o_copy(n_rows - 1).wait()

    @pl.when(e == n_exp - 1)
    def _():
        used = first + n_rows
        total = o_hbm.shape[0] // tm
        obuf[0] = jnp.zeros(obuf.shape[1:], obuf.dtype)

        def start(tile, carry):
            tile_store(tile, 0).start()
            return carry

        def wait(tile, carry):
            tile_store(tile, 0).wait()
            return carry

        lax.fori_loop(used, total, start, 0)
        lax.fori_loop(used, total, wait, 0)

    after = step + 2

    @pl.when(after < n_col * n_exp)
    def _():
        start_weights(after // n_exp, lax.rem(after, n_exp))

    @pl.when(has_next & (nt_ref[e_nxt] > 0))
    def _():
        tile_load(ts_ref[e_nxt], 0).start()


def _expert_matmul(n_tiles, tile_start, x, w, b, *, parts, epilogue, out_dtype, name):
    n_exp, k, cols = w.shape
    n = cols // parts
    tn = EXPERT_COLS // parts
    n_col = n // tn
    bias = lambda p: pl.BlockSpec((None, 1, tn), lambda j, e, nt, ts: (e, 0, p * n_col + j))
    any_space = pl.BlockSpec(memory_space=pl.ANY)
    return pl.pallas_call(
        functools.partial(_expert_matmul_kernel, parts=parts, part_stride=n, epilogue=epilogue),
        grid_spec=pltpu.PrefetchScalarGridSpec(
            num_scalar_prefetch=2,
            grid=(n_col, n_exp),
            in_specs=[any_space, any_space] + [bias(p) for p in range(parts)],
            out_specs=any_space,
            scratch_shapes=[
                pltpu.VMEM((2, MOE_TILE, k), BF16),
                pltpu.VMEM((2, MOE_TILE, tn), out_dtype),
                pltpu.VMEM((2, parts, k, tn), BF16),
                pltpu.VMEM((STAGE_SLOTS, parts, WEIGHT_CHUNK, tn), F32),
                pltpu.SemaphoreType.DMA((2,)),
                pltpu.SemaphoreType.DMA((2,)),
                pltpu.SemaphoreType.DMA((STAGE_SLOTS,)),
            ],
        ),
        out_shape=jax.ShapeDtypeStruct((x.shape[0], n), out_dtype),
        compiler_params=_params("arbitrary", "arbitrary"),
        name=name,
    )(n_tiles, tile_start, x, w, *([b] * parts))


def _combine_kernel(pos_ref, pos_next_ref, y_ref, gate_ref, g_ref, ys_hbm, o_ref, buf, sem):
    i = pl.program_id(0)
    slot = lax.rem(i, 2)
    per_iter = DMA_UNROLL // TOP_K

    def request(idx_ref, dst):
        def start(it, c):
            for u in range(per_iter):
                t = it * per_iter + u
                for k in range(TOP_K):
                    _row_copy(ys_hbm, idx_ref[0, 0, t * TOP_K + k], buf.at[dst], k * COMBINE_TILE + t,
                              sem.at[dst]).start()
            return c

        lax.fori_loop(0, COMBINE_TILE // per_iter, start, 0)

    @pl.when(i == 0)
    def _():
        request(pos_ref, 0)

    @pl.when(i + 1 < pl.num_programs(0))
    def _():
        request(pos_next_ref, 1 - slot)

    def wait(it, c):
        for u in range(DMA_UNROLL):
            _row_copy(ys_hbm, 0, buf.at[slot], it * DMA_UNROLL + u, sem.at[slot]).wait()
        return c

    lax.fori_loop(0, TOP_K * COMBINE_TILE // DMA_UNROLL, wait, 0)
    gates = gate_ref[...]
    rows = buf.at[slot]
    moe = gates[:, 0:1] * rows[0:COMBINE_TILE, :]
    for k in range(1, TOP_K):
        moe = moe + gates[:, k:k + 1] * rows[k * COMBINE_TILE:(k + 1) * COMBINE_TILE, :]
    o_ref[...] = _rms(y_ref[...] + moe, g_ref[...])


def _combine(pos, y, gates, g, ys, first_row, n_rows, name):
    off = first_row // COMBINE_TILE
    n_tiles = n_rows // COMBINE_TILE
    tok = lambda width: pl.BlockSpec((COMBINE_TILE, width), lambda i: (off + i, 0))
    idx = lambda step: pl.BlockSpec(
        (1, 1, TOP_K * COMBINE_TILE), lambda i: (off + jnp.minimum(i + step, n_tiles - 1), 0, 0),
        memory_space=pltpu.SMEM)
    pos = pos.reshape(N_TOK // COMBINE_TILE, 1, TOP_K * COMBINE_TILE)
    return pl.pallas_call(
        _combine_kernel,
        grid=(n_tiles,),
        in_specs=[idx(0), idx(1), tok(D_MODEL), tok(LANES),
                  pl.BlockSpec((1, D_MODEL), lambda i: (0, 0)),
                  pl.BlockSpec(memory_space=pl.ANY)],
        out_specs=pl.BlockSpec((COMBINE_TILE, D_MODEL), lambda i: (i, 0)),
        out_shape=jax.ShapeDtypeStruct((n_rows, D_MODEL), F32),
        scratch_shapes=[pltpu.VMEM((2, TOP_K * COMBINE_TILE, D_MODEL), F32), pltpu.SemaphoreType.DMA((2,))],
        compiler_params=_params("arbitrary"),
        name=name,
    )(pos, pos, y, gates, g, ys)


def kernel(x_prompt, x_sample, cache_attn_k, cache_attn_v, state_conv, norm_attn, w_in, conv_w,
           rel_table, norm_att_out, norm_conv_out, w_out, norm_ffn, w_router, b_router,
           w_gate_up, b_gate_up, w_down, b_down, norm_final):
    row = lambda g: g.reshape(1, -1)
    x_p = x_prompt.reshape(N_PROMPT, D_MODEL)
    x_s = x_sample.reshape(N_SAMPLE, D_MODEL)

    w_in_bf = w_in[0].astype(BF16)
    p_p = _norm_matmul(x_p, row(norm_attn[0]), w_in_bf, 1024, "norm_in_proj_prompt")
    p_s = _norm_matmul(x_s, row(norm_attn[0]), w_in_bf, 1024, "norm_in_proj_sample")

    r = cache_attn_k.shape[2]
    att_p = _attn_prompt(p_p, _prompt_bias(rel_table[0]))
    bias_c, bias_n = _sample_bias(rel_table[0], r)
    att_s = _attn_sample(p_s, cache_attn_k.reshape(DEC_BATCH, r, MIX_ATT),
                         cache_attn_v.reshape(DEC_BATCH, r, MIX_ATT), bias_c, bias_n)

    st = state_conv[0]
    pad = lambda a: jnp.pad(a, ((0, 0), (0, DEC_SEQ - a.shape[1]), (0, 0))).reshape(N_SAMPLE, MIX_CONV)
    merged, z = _mix(att_p, p_p, att_s, p_s, pad(st[:, 1:2]), pad(st), conv_w[0],
                     row(norm_att_out[0]), row(norm_conv_out[0]))
    y1 = _matmul_residual(merged, w_out[0].astype(BF16), x_p, x_s, 1024)

    w_r = jnp.pad(w_router[0], ((0, 0), (0, LANES - N_EXPERTS))).astype(BF16)
    b_r = jnp.pad(b_router[0], (0, LANES - N_EXPERTS)).reshape(1, LANES)
    h, idx, gates = _router(y1, row(norm_ffn[0]), w_r, b_r)

    pos, src, n_tiles, tile_start, tile_rows = _routing(idx[:, :TOP_K])
    xs = _gather(tile_rows, src, h)
    hs = _expert_matmul(n_tiles, tile_start, xs, w_gate_up.reshape(N_EXPERTS, D_MODEL, 2 * D_FF),
                        b_gate_up.reshape(N_EXPERTS, 1, 2 * D_FF),
                        parts=2, epilogue=_swiglu, out_dtype=BF16, name="expert_gate_up")
    ys = _expert_matmul(n_tiles, tile_start, hs, w_down.reshape(N_EXPERTS, D_FF, D_MODEL),
                        b_down.reshape(N_EXPERTS, 1, D_MODEL),
                        parts=1, epilogue=lambda a: a, out_dtype=F32, name="expert_down")
    y_p = _combine(pos, y1, gates, row(norm_final), ys, 0, N_PROMPT, "combine_final_norm_prompt")
    y_s = _combine(pos, y1, gates, row(norm_final), ys, N_PROMPT, N_SAMPLE, "combine_final_norm_sample")

    keep = min(BAND_ROWS, SEQ)
    shape_p = (1, 1, keep, N_HEADS, HEAD_DIM)
    shape_s = (1, DEC_BATCH, DEC_SEQ, N_HEADS, HEAD_DIM)
    z_s = z[N_PROMPT:].reshape(DEC_BATCH, DEC_SEQ, MIX_CONV)
    return (y_p.reshape(1, SEQ, D_MODEL),
            y_s.reshape(DEC_BATCH, DEC_SEQ, D_MODEL),
            p_p[N_PROMPT - keep:, MIX_ATT:2 * MIX_ATT].reshape(shape_p),
            p_p[N_PROMPT - keep:, 2 * MIX_ATT:3 * MIX_ATT].reshape(shape_p),
            z[N_PROMPT - 2:N_PROMPT].reshape(1, 1, 2, MIX_CONV),
            p_s[:, MIX_ATT:2 * MIX_ATT].reshape(shape_s),
            p_s[:, 2 * MIX_ATT:3 * MIX_ATT].reshape(shape_s),
            z_s[:, DEC_SEQ - 2:].reshape(1, DEC_BATCH, 2, MIX_CONV))
```

```python
import functools

import jax
import jax.numpy as jnp
import numpy as np
from jax import lax
from jax.experimental import pallas as pl
from jax.experimental.pallas import tpu as pltpu

D_MODEL = 4096
SEQ = 16384
DEC_BATCH = 32
DEC_SEQ = 16
CHUNK = 64
LEFT_CHUNKS = 8
BAND_ROWS = LEFT_CHUNKS * CHUNK
MIX_ATT = 2048
MIX_CONV = 2048
HEAD_DIM = 128
N_HEADS = 16
MAX_REL = 4 * CHUNK
IN_COLS = 3 * MIX_ATT + 3 * MIX_CONV
N_EXPERTS = 32
TOP_K = 4
D_FF = D_MODEL
SWIGLU_LIMIT = 7.0
SWIGLU_ALPHA = 1.702
EPS = 1e-5
SCALE = HEAD_DIM ** -0.5

N_PROMPT = SEQ
N_SAMPLE = DEC_BATCH * DEC_SEQ
N_TOK = N_PROMPT + N_SAMPLE
N_PAIRS = N_TOK * TOP_K

LANES = 128
ROW_TILE = 512
N_ROW_TILES = N_TOK // ROW_TILE
N_PROMPT_TILES = N_PROMPT // ROW_TILE
Q_ROWS = 2 * CHUNK
BAND_KEYS = BAND_ROWS + Q_ROWS
HALO_ROWS = 8
MIX_TILE = 128
MIX_PROMPT_TILES = N_PROMPT // MIX_TILE

MOE_TILE = 272
MOE_TILES = -(-N_PAIRS // MOE_TILE) + N_EXPERTS
MOE_ROWS = MOE_TILES * MOE_TILE
EXPERT_COLS = 2048
WEIGHT_CHUNK = 512
STAGE_SLOTS = 2
DMA_UNROLL = 8
COMBINE_TILE = 128

VMEM_LIMIT = 56 * 1024 * 1024

BF16 = jnp.bfloat16
F32 = jnp.float32
NEG_INF = float("-inf")


def _params(*sem):
    return pltpu.CompilerParams(dimension_semantics=sem, vmem_limit_bytes=VMEM_LIMIT)


def _rms(x, g):
    return x * lax.rsqrt(jnp.mean(x * x, axis=-1, keepdims=True) + EPS) * g


def _norm_matmul_kernel(x_ref, g_ref, w_ref, o_ref, h_ref):
    @pl.when(pl.program_id(1) == 0)
    def _():
        h_ref[...] = _rms(x_ref[...], g_ref[...]).astype(BF16)

    o_ref[...] = jnp.dot(h_ref[...], w_ref[...], preferred_element_type=F32)


def _norm_matmul(x, g, w, tn, name):
    m, k = x.shape
    n = w.shape[1]
    return pl.pallas_call(
        _norm_matmul_kernel,
        grid=(m // ROW_TILE, n // tn),
        in_specs=[
            pl.BlockSpec((ROW_TILE, k), lambda i, j: (i, 0)),
            pl.BlockSpec((1, k), lambda i, j: (0, 0)),
            pl.BlockSpec((k, tn), lambda i, j: (0, j)),
        ],
        out_specs=pl.BlockSpec((ROW_TILE, tn), lambda i, j: (i, j)),
        out_shape=jax.ShapeDtypeStruct((m, n), F32),
        scratch_shapes=[pltpu.VMEM((ROW_TILE, k), BF16)],
        compiler_params=_params("parallel", "arbitrary"),
        name=name,
    )(x, g, w)


def _attn_prompt_kernel(q_ref, kp_ref, kc_ref, vp_ref, vc_ref, b_ref, o_ref):
    m = pl.program_id(1)
    k = jnp.concatenate([kp_ref[...], kc_ref[...]], axis=0).astype(BF16)
    v = jnp.concatenate([vp_ref[...], vc_ref[...]], axis=0).astype(BF16)
    col = lax.broadcasted_iota(jnp.int32, (Q_ROWS, BAND_KEYS), 1)
    starts = range(0, ROW_TILE, Q_ROWS)
    scores = []
    for lo in starts:
        q = q_ref[lo:lo + Q_ROWS, :].astype(BF16)
        s = lax.dot_general(q, k[lo:lo + BAND_KEYS], (((1,), (1,)), ((), ())), preferred_element_type=F32)
        s = s * SCALE + b_ref[0]
        scores.append(jnp.where(col < jnp.where(m == 0, ROW_TILE - lo, 0), NEG_INF, s))
    probs = []
    for s in scores:
        e = jnp.exp(s - jnp.max(s, axis=-1, keepdims=True))
        probs.append((e / jnp.sum(e, axis=-1, keepdims=True)).astype(BF16))
    for lo, p in zip(starts, probs):
        o_ref[lo:lo + Q_ROWS, :] = jnp.dot(p, v[lo:lo + BAND_KEYS], preferred_element_type=F32)


def _attn_prompt(p_all, bias):
    blk = (ROW_TILE, HEAD_DIM)
    prev = lambda h, m: jnp.maximum(m - 1, 0)
    return pl.pallas_call(
        _attn_prompt_kernel,
        grid=(N_HEADS, N_PROMPT_TILES),
        in_specs=[
            pl.BlockSpec(blk, lambda h, m: (m, h)),
            pl.BlockSpec(blk, lambda h, m: (prev(h, m), N_HEADS + h)),
            pl.BlockSpec(blk, lambda h, m: (m, N_HEADS + h)),
            pl.BlockSpec(blk, lambda h, m: (prev(h, m), 2 * N_HEADS + h)),
            pl.BlockSpec(blk, lambda h, m: (m, 2 * N_HEADS + h)),
            pl.BlockSpec((1, Q_ROWS, BAND_KEYS), lambda h, m: (h, 0, 0)),
        ],
        out_specs=pl.BlockSpec(blk, lambda h, m: (m, h)),
        out_shape=jax.ShapeDtypeStruct((N_PROMPT, MIX_ATT), F32),
        compiler_params=_params("parallel", "arbitrary"),
        name="attn_prompt",
    )(p_all, p_all, p_all, p_all, p_all, bias)


def _prompt_bias(table):
    n_lags = Q_ROWS + BAND_KEYS - 1
    rel = BAND_ROWS + (Q_ROWS - 1 - jnp.arange(n_lags))
    per_lag = table[:, jnp.clip(rel, -MAX_REL, MAX_REL) + MAX_REL]
    period = jnp.pad(per_lag, ((0, 0), (0, 1)))
    skew = jnp.tile(period, (1, Q_ROWS))[:, :Q_ROWS * n_lags].reshape(N_HEADS, Q_ROWS, n_lags)
    toeplitz = skew[:, :, Q_ROWS - 1:Q_ROWS - 1 + BAND_KEYS]
    i = np.arange(Q_ROWS)[:, None]
    j = np.arange(BAND_KEYS)[None, :]
    chunk_lag = (LEFT_CHUNKS + i // CHUNK) - j // CHUNK
    band = (chunk_lag >= 0) & (chunk_lag <= LEFT_CHUNKS)
    return jnp.where(band[None], toeplitz, NEG_INF).astype(F32)


def _attn_sample_kernel(q_ref, k_ref, v_ref, ck_ref, cv_ref, bc_ref, bn_ref, o_ref):
    tb = (((1,), (1,)), ((), ()))
    heads = [slice(h * HEAD_DIM, (h + 1) * HEAD_DIM) for h in range(N_HEADS)]
    scores = []
    for h, sl in enumerate(heads):
        q = q_ref[:, sl].astype(BF16)
        sc = lax.dot_general(q, ck_ref[0, :, sl].astype(BF16), tb, preferred_element_type=F32)
        sn = lax.dot_general(q, k_ref[:, sl].astype(BF16), tb, preferred_element_type=F32)
        scores.append((sc * SCALE + bc_ref[h], sn * SCALE + bn_ref[h]))
    probs = []
    for sc, sn in scores:
        mx = jnp.maximum(jnp.max(sc, axis=-1, keepdims=True), jnp.max(sn, axis=-1, keepdims=True))
        ec = jnp.exp(sc - mx)
        en = jnp.exp(sn - mx)
        tot = jnp.sum(ec, axis=-1, keepdims=True) + jnp.sum(en, axis=-1, keepdims=True)
        probs.append(((ec / tot).astype(BF16), (en / tot).astype(BF16)))
    for sl, (pc, pn) in zip(heads, probs):
        o_ref[:, sl] = (jnp.dot(pc, cv_ref[0, :, sl].astype(BF16), preferred_element_type=F32)
                        + jnp.dot(pn, v_ref[:, sl].astype(BF16), preferred_element_type=F32))


def _attn_sample(p_s, cache_k, cache_v, bias_c, bias_n):
    r = cache_k.shape[1]
    new = lambda c: pl.BlockSpec((DEC_SEQ, MIX_ATT), lambda b: (b, c))
    cache = pl.BlockSpec((1, r, MIX_ATT), lambda b: (b, 0, 0))
    return pl.pallas_call(
        _attn_sample_kernel,
        grid=(DEC_BATCH,),
        in_specs=[new(0), new(1), new(2), cache, cache,
                  pl.BlockSpec((N_HEADS, DEC_SEQ, r), lambda b: (0, 0, 0)),
                  pl.BlockSpec((N_HEADS, DEC_SEQ, DEC_SEQ), lambda b: (0, 0, 0))],
        out_specs=pl.BlockSpec((DEC_SEQ, MIX_ATT), lambda b: (b, 0)),
        out_shape=jax.ShapeDtypeStruct((N_SAMPLE, MIX_ATT), F32),
        compiler_params=_params("parallel"),
        name="attn_sample",
    )(p_s, p_s, p_s, cache_k, cache_v, bias_c, bias_n)


def _sample_bias(table, r):
    rel = (r + jnp.arange(DEC_SEQ))[:, None] - jnp.arange(r + DEC_SEQ)[None, :]
    bias = table[:, jnp.clip(rel, -MAX_REL, MAX_REL) + MAX_REL].astype(F32)
    return bias[:, :, :r], bias[:, :, r:]


def _mix_kernel(ap_ref, bgp_ref, cgp_ref, up_ref, cgh_ref, uh_ref, as_ref, bgs_ref, cgs_ref, us_ref,
                s1_ref, s2_ref, cw_ref, ga_ref, gc_ref, o_ref, z_ref):
    i = pl.program_id(0)
    sample = i >= MIX_PROMPT_TILES
    z = jnp.where(sample, cgs_ref[...] * us_ref[...], cgp_ref[...] * up_ref[...])
    z_ref[...] = z
    zh = jnp.where(i > 0, cgh_ref[...] * uh_ref[...], 0.0)
    row = lax.broadcasted_iota(jnp.int32, z.shape, 0)
    pos = jnp.where(sample, row & (DEC_SEQ - 1), row)
    z1 = jnp.where(pos == 0, jnp.where(sample, s1_ref[...], zh[7:8]), pltpu.roll(z, 1, 0))
    z2 = jnp.where(pos < 2,
                   jnp.where(sample, s2_ref[...], jnp.where(row == 0, zh[6:7], zh[7:8])),
                   pltpu.roll(z, 2, 0))
    y = cw_ref[0:1] * z2
    y = y + cw_ref[1:2] * z1
    y = y + cw_ref[2:3] * z
    att = jnp.where(sample, as_ref[...], ap_ref[...])
    bg = jnp.where(sample, bgs_ref[...], bgp_ref[...])
    o_ref[:, :MIX_ATT] = _rms(att, ga_ref[...]).astype(BF16)
    o_ref[:, MIX_ATT:] = _rms(bg * y, gc_ref[...]).astype(BF16)


def _mix(att_p, p_p, att_s, p_s, s1, s2, conv_w, g_att, g_conv):
    prompt = lambda c: pl.BlockSpec((MIX_TILE, MIX_CONV), lambda i: (jnp.minimum(i, MIX_PROMPT_TILES - 1), c))
    sample = lambda c: pl.BlockSpec((MIX_TILE, MIX_CONV), lambda i: (jnp.maximum(i - MIX_PROMPT_TILES, 0), c))
    halo = lambda c: pl.BlockSpec(
        (HALO_ROWS, MIX_CONV),
        lambda i: (jnp.clip(i * (MIX_TILE // HALO_ROWS) - 1, 0, N_PROMPT // HALO_ROWS - 1), c))
    const = lambda shape: pl.BlockSpec(shape, lambda i: (0, 0))
    return pl.pallas_call(
        _mix_kernel,
        grid=(N_TOK // MIX_TILE,),
        in_specs=[
            prompt(0), prompt(3), prompt(4), prompt(5), halo(4), halo(5),
            sample(0), sample(3), sample(4), sample(5),
            sample(0), sample(0),
            const((3, MIX_CONV)), const((1, MIX_ATT)), const((1, MIX_CONV)),
        ],
        out_specs=[pl.BlockSpec((MIX_TILE, D_MODEL), lambda i: (i, 0)),
                   pl.BlockSpec((MIX_TILE, MIX_CONV), lambda i: (i, 0))],
        out_shape=[jax.ShapeDtypeStruct((N_TOK, D_MODEL), BF16),
                   jax.ShapeDtypeStruct((N_TOK, MIX_CONV), F32)],
        compiler_params=_params("parallel"),
        name="conv_norms",
    )(att_p, p_p, p_p, p_p, p_p, p_p, att_s, p_s, p_s, p_s, s1, s2, conv_w, g_att, g_conv)


def _matmul_residual_kernel(a_ref, w_ref, rp_ref, rs_ref, o_ref):
    res = jnp.where(pl.program_id(0) < N_PROMPT_TILES, rp_ref[...], rs_ref[...])
    o_ref[...] = res + jnp.dot(a_ref[...], w_ref[...], preferred_element_type=F32)


def _matmul_residual(a, w, res_p, res_s, tn):
    m, k = a.shape
    n = w.shape[1]
    return pl.pallas_call(
        _matmul_residual_kernel,
        grid=(m // ROW_TILE, n // tn),
        in_specs=[
            pl.BlockSpec((ROW_TILE, k), lambda i, j: (i, 0)),
            pl.BlockSpec((k, tn), lambda i, j: (0, j)),
            pl.BlockSpec((ROW_TILE, tn), lambda i, j: (jnp.minimum(i, N_PROMPT_TILES - 1), j)),
            pl.BlockSpec((ROW_TILE, tn), lambda i, j: (0, j)),
        ],
        out_specs=pl.BlockSpec((ROW_TILE, tn), lambda i, j: (i, j)),
        out_shape=jax.ShapeDtypeStruct((m, n), F32),
        compiler_params=_params("parallel", "arbitrary"),
        name="out_proj",
    )(a, w, res_p, res_s)


def _router_kernel(y_ref, g_ref, w_ref, b_ref, h_ref, idx_ref, gate_ref):
    h = _rms(y_ref[...], g_ref[...])
    h_ref[...] = h
    logits = jnp.dot(h.astype(BF16), w_ref[...], preferred_element_type=F32) + b_ref[...]
    lane = lax.broadcasted_iota(jnp.int32, logits.shape, 1).astype(F32)
    left = jnp.where(lane < N_EXPERTS, logits, NEG_INF)
    vals, idxs = [], []
    for _ in range(TOP_K):
        top = jnp.max(left, axis=-1, keepdims=True)
        pick = jnp.min(jnp.where(left == top, lane, float(LANES)), axis=-1, keepdims=True)
        vals.append(top)
        idxs.append(pick)
        left = jnp.where(lane == pick, NEG_INF, left)
    exps = [jnp.exp(v - vals[0]) for v in vals]
    tot = exps[0]
    for e in exps[1:]:
        tot = tot + e
    idx_out = jnp.zeros_like(logits)
    gate_out = jnp.zeros_like(logits)
    for k in range(TOP_K):
        idx_out = jnp.where(lane == k, idxs[k], idx_out)
        gate_out = jnp.where(lane == k, exps[k] / tot, gate_out)
    idx_ref[...] = idx_out.astype(jnp.int32)
    gate_ref[...] = gate_out


def _router(y, g, w, b):
    const = lambda shape: pl.BlockSpec(shape, lambda i: (0, 0))
    return pl.pallas_call(
        _router_kernel,
        grid=(N_ROW_TILES,),
        in_specs=[pl.BlockSpec((ROW_TILE, D_MODEL), lambda i: (i, 0)),
                  const((1, D_MODEL)), const((D_MODEL, LANES)), const((1, LANES))],
        out_specs=[pl.BlockSpec((ROW_TILE, D_MODEL), lambda i: (i, 0)),
                   pl.BlockSpec((ROW_TILE, LANES), lambda i: (i, 0)),
                   pl.BlockSpec((ROW_TILE, LANES), lambda i: (i, 0))],
        out_shape=[jax.ShapeDtypeStruct((N_TOK, D_MODEL), F32),
                   jax.ShapeDtypeStruct((N_TOK, LANES), jnp.int32),
                   jax.ShapeDtypeStruct((N_TOK, LANES), F32)],
        compiler_params=_params("parallel"),
        name="ffn_norm_router",
    )(y, g, w, b)


def _routing(top_idx):
    e_flat = top_idx.reshape(-1)
    onehot = (e_flat[:, None] == jnp.arange(N_EXPERTS)[None, :]).astype(jnp.int32)
    csum = jnp.cumsum(onehot, axis=0)
    rank = jnp.take_along_axis(csum, e_flat[:, None], axis=1)[:, 0] - 1
    count = csum[-1]
    n_tiles = (count + MOE_TILE - 1) // MOE_TILE
    tile_end = jnp.cumsum(n_tiles)
    tile_start = tile_end - n_tiles
    pos = tile_start[e_flat] * MOE_TILE + rank
    src = jnp.zeros((MOE_ROWS,), jnp.int32).at[pos].set(jnp.arange(N_PAIRS, dtype=jnp.int32) // TOP_K)
    tile = jnp.arange(MOE_TILES, dtype=jnp.int32)
    owner = jnp.minimum(jnp.sum(tile_end[None, :] <= tile[:, None], axis=1), N_EXPERTS - 1)
    tile_rows = jnp.clip(count[owner] - (tile - tile_start[owner]) * MOE_TILE, 0, MOE_TILE)
    i32 = lambda a: a.astype(jnp.int32)
    return i32(pos), src, i32(n_tiles), i32(tile_start), i32(tile_rows)


def _row_copy(src_hbm, row, buf, slot, sem):
    return pltpu.make_async_copy(src_hbm.at[pl.ds(row, 1), :], buf.at[pl.ds(slot, 1), :], sem)


def _gather_kernel(rows_ref, src_ref, src_next_ref, h_hbm, o_ref, buf, sem):
    i = pl.program_id(0)
    slot = lax.rem(i, 2)

    def groups(tile):
        return (rows_ref[tile] + DMA_UNROLL - 1) // DMA_UNROLL

    def request(tile, idx_ref, dst):
        def start(g, c):
            for u in range(DMA_UNROLL):
                r = g * DMA_UNROLL + u
                _row_copy(h_hbm, idx_ref[0, 0, r], buf.at[dst], r, sem.at[dst]).start()
            return c

        lax.fori_loop(0, groups(tile), start, 0)

    @pl.when(i == 0)
    def _():
        buf[...] = jnp.zeros_like(buf)
        request(0, src_ref, 0)

    @pl.when(i + 1 < pl.num_programs(0))
    def _():
        request(i + 1, src_next_ref, 1 - slot)

    def wait(g, c):
        for u in range(DMA_UNROLL):
            _row_copy(h_hbm, 0, buf.at[slot], g * DMA_UNROLL + u, sem.at[slot]).wait()
        return c

    lax.fori_loop(0, groups(i), wait, 0)
    o_ref[...] = buf[slot].astype(BF16)


def _gather(tile_rows, src, h):
    src = src.reshape(MOE_TILES, 1, MOE_TILE)
    idx = lambda step: pl.BlockSpec(
        (1, 1, MOE_TILE), lambda i, rows: (jnp.minimum(i + step, MOE_TILES - 1), 0, 0), memory_space=pltpu.SMEM)
    return pl.pallas_call(
        _gather_kernel,
        grid_spec=pltpu.PrefetchScalarGridSpec(
            num_scalar_prefetch=1,
            grid=(MOE_TILES,),
            in_specs=[idx(0), idx(1), pl.BlockSpec(memory_space=pl.ANY)],
            out_specs=pl.BlockSpec((MOE_TILE, D_MODEL), lambda i, rows: (i, 0)),
            scratch_shapes=[pltpu.VMEM((2, MOE_TILE, D_MODEL), F32), pltpu.SemaphoreType.DMA((2,))],
        ),
        out_shape=jax.ShapeDtypeStruct((MOE_ROWS, D_MODEL), BF16),
        compiler_params=_params("arbitrary"),
        name="gather_rows",
    )(tile_rows, src, src, h)


def _swiglu(gate, up):
    gate = jnp.minimum(gate, SWIGLU_LIMIT)
    up = jnp.clip(up, -SWIGLU_LIMIT, SWIGLU_LIMIT)
    return (up + 1.0) * (gate * jax.nn.sigmoid(SWIGLU_ALPHA * gate))


def _expert_matmul_kernel(nt_ref, ts_ref, x_hbm, w_hbm, *refs, parts, part_stride, epilogue):
    b_refs = refs[:parts]
    o_hbm, xbuf, obuf, wbf, stage, xsem, osem, ssem, state = refs[parts:]
    tm, tn, ch = xbuf.shape[1], wbf.shape[3], stage.shape[2]
    n_chunks = wbf.shape[2] // ch
    n_col, n_exp = pl.num_programs(0), pl.num_programs(1)
    j, e = pl.program_id(0), pl.program_id(1)
    step = j * n_exp + e
    cur = lax.rem(step, 2)
    has_next = step + 1 < n_col * n_exp
    nxt_step = jnp.minimum(step + 1, n_col * n_exp - 1)
    j_nxt, e_nxt = nxt_step // n_exp, lax.rem(nxt_step, n_exp)
    n_rows, first = nt_ref[e], ts_ref[e]

    @pl.when(step == 0)
    def _():
        for s in range(3):
            state[s] = 0

    def chunk_copies(jj, ee, c):
        slot = lax.rem(c, STAGE_SLOTS)
        rows = pl.ds(pl.multiple_of(c * ch, ch), ch)
        return [pltpu.make_async_copy(
            w_hbm.at[ee, rows, pl.ds(pl.multiple_of(jj * tn + p * part_stride, tn), tn)],
            stage.at[slot, p], ssem.at[slot]) for p in range(parts)]

    def tile_load(tile, slot):
        return pltpu.make_async_copy(
            x_hbm.at[pl.ds(pl.multiple_of(tile * tm, tm), tm), :], xbuf.at[slot], xsem.at[slot])

    def x_copy(t):
        return tile_load(first + t, lax.rem(t, 2))

    def tile_store(tile, slot):
        return pltpu.make_async_copy(
            obuf.at[slot],
            o_hbm.at[pl.ds(pl.multiple_of(tile * tm, tm), tm), pl.ds(pl.multiple_of(j * tn, tn), tn)],
            osem.at[slot])

    tiles_done = state[2]

    def out_slot(t):
        return lax.rem(tiles_done + t, 2)

    def finish_stores():
        for s in range(2):
            @pl.when(state[s] == 1)
            def _():
                tile_store(0, s).wait()
                state[s] = 0

    def start_weights(jj, ee):
        for c in range(STAGE_SLOTS - 1):
            for cp in chunk_copies(jj, ee, c):
                cp.start()

    def chunk_arrive(jj, ee, c):
        @pl.when(c + STAGE_SLOTS - 1 < n_chunks)
        def _():
            for cp in chunk_copies(jj, ee, c + STAGE_SLOTS - 1):
                cp.start()

        for cp in chunk_copies(jj, ee, c):
            cp.wait()

    def chunk_convert(c, dst):
        slot = lax.rem(c, STAGE_SLOTS)
        rows = pl.ds(pl.multiple_of(c * ch, ch), ch)
        for p in range(parts):
            wbf[dst, p, rows, :] = stage[slot, p].astype(BF16)

    def rows_arrive(t):
        @pl.when(t + 1 < n_rows)
        def _():
            x_copy(t + 1).start()

        @pl.when(state[out_slot(t)] == 1)
        def _():
            tile_store(0, out_slot(t)).wait()

        x_copy(t).wait()

    def rows_compute(t, between=lambda: None):
        x = xbuf[lax.rem(t, 2)]
        slot = out_slot(t)
        if parts > 1:
            head = jnp.dot(x, wbf[cur, 0], preferred_element_type=F32) + b_refs[0][...]
            between()
            rest = [jnp.dot(x, wbf[cur, p], preferred_element_type=F32) + b_refs[p][...] for p in range(1, parts)]
            obuf[slot] = epilogue(head, *rest).astype(obuf.dtype)
        else:
            half = tn // 2
            head = jnp.dot(x, wbf[cur, 0, :, :half], preferred_element_type=F32) + b_refs[0][:, :half]
            between()
            obuf[slot, :, :half] = epilogue(head).astype(obuf.dtype)
            tail = jnp.dot(x, wbf[cur, 0, :, half:], preferred_element_type=F32) + b_refs[0][:, half:]
            obuf[slot, :, half:] = epilogue(tail).astype(obuf.dtype)
        tile_store(first + t, slot).start()
        state[slot] = 1

    @pl.when(step == 0)
    def _():
        start_weights(0, 0)

        def load(c, carry):
            chunk_arrive(0, 0, c)
            chunk_convert(c, 0)
            return carry

        lax.fori_loop(0, n_chunks, load, 0)

        @pl.when(has_next)
        def _():
            start_weights(j_nxt, e_nxt)

        @pl.when(n_rows > 0)
        def _():
            x_copy(0).start()

    todo = jnp.where(has_next, n_chunks, 0)

    def rows_and_chunk(t, carry):
        rows_arrive(t)

        @pl.when(t < todo)
        def _():
            chunk_arrive(j_nxt, e_nxt, t)

        rows_compute(t, lambda: chunk_convert(jnp.minimum(t, n_chunks - 1), 1 - cur))
        return carry

    def chunk_only(c, carry):
        chunk_arrive(j_nxt, e_nxt, c)
        chunk_convert(c, 1 - cur)
        return carry

    lax.fori_loop(0, n_rows, rows_and_chunk, 0)
    lax.fori_loop(jnp.minimum(n_rows, todo), todo, chunk_only, 0)

    state[2] = tiles_done + n_rows

    @pl.when(e == n_exp - 1)
    def _():
        finish_stores()
        used = first + n_rows
        total = o_hbm.shape[0] // tm
        obuf[0] = jnp.zeros(obuf.shape[1:], obuf.dtype)

        def start(tile, carry):
            tile_store(tile, 0).start()
            return carry

        def wait(tile, carry):
            tile_store(tile, 0).wait()
            return carry

        lax.fori_loop(used, total, start, 0)
        lax.fori_loop(used, total, wait, 0)

    after = step + 2

    @pl.when(after < n_col * n_exp)
    def _():
        start_weights(after // n_exp, lax.rem(after, n_exp))

    @pl.when(has_next & (nt_ref[e_nxt] > 0))
    def _():
        tile_load(ts_ref[e_nxt], 0).start()


def _expert_matmul(n_tiles, tile_start, x, w, b, *, parts, epilogue, out_dtype, name):
    n_exp, k, cols = w.shape
    n = cols // parts
    tn = EXPERT_COLS // parts
    n_col = n // tn
    bias = lambda p: pl.BlockSpec((None, 1, tn), lambda j, e, nt, ts: (e, 0, p * n_col + j))
    any_space = pl.BlockSpec(memory_space=pl.ANY)
    return pl.pallas_call(
        functools.partial(_expert_matmul_kernel, parts=parts, part_stride=n, epilogue=epilogue),
        grid_spec=pltpu.PrefetchScalarGridSpec(
            num_scalar_prefetch=2,
            grid=(n_col, n_exp),
            in_specs=[any_space, any_space] + [bias(p) for p in range(parts)],
            out_specs=any_space,
            scratch_shapes=[
                pltpu.VMEM((2, MOE_TILE, k), BF16),
                pltpu.VMEM((2, MOE_TILE, tn), out_dtype),
                pltpu.VMEM((2, parts, k, tn), BF16),
                pltpu.VMEM((STAGE_SLOTS, parts, WEIGHT_CHUNK, tn), F32),
                pltpu.SemaphoreType.DMA((2,)),
                pltpu.SemaphoreType.DMA((2,)),
                pltpu.SemaphoreType.DMA((STAGE_SLOTS,)),
                pltpu.SMEM((3,), jnp.int32),
            ],
        ),
        out_shape=jax.ShapeDtypeStruct((x.shape[0], n), out_dtype),
        compiler_params=_params("arbitrary", "arbitrary"),
        name=name,
    )(n_tiles, tile_start, x, w, *([b] * parts))


def _combine_kernel(pos_ref, pos_next_ref, y_ref, gate_ref, g_ref, ys_hbm, o_ref, buf, sem):
    i = pl.program_id(0)
    slot = lax.rem(i, 2)
    per_iter = DMA_UNROLL // TOP_K

    def request(idx_ref, dst):
        def start(it, c):
            for u in range(per_iter):
                t = it * per_iter + u
                for k in range(TOP_K):
                    _row_copy(ys_hbm, idx_ref[0, 0, t * TOP_K + k], buf.at[dst], k * COMBINE_TILE + t,
                              sem.at[dst]).start()
            return c

        lax.fori_loop(0, COMBINE_TILE // per_iter, start, 0)

    @pl.when(i == 0)
    def _():
        request(pos_ref, 0)

    @pl.when(i + 1 < pl.num_programs(0))
    def _():
        request(pos_next_ref, 1 - slot)

    def wait(it, c):
        for u in range(DMA_UNROLL):
            _row_copy(ys_hbm, 0, buf.at[slot], it * DMA_UNROLL + u, sem.at[slot]).wait()
        return c

    lax.fori_loop(0, TOP_K * COMBINE_TILE // DMA_UNROLL, wait, 0)
    gates = gate_ref[...]
    rows = buf.at[slot]
    moe = gates[:, 0:1] * rows[0:COMBINE_TILE, :]
    for k in range(1, TOP_K):
        moe = moe + gates[:, k:k + 1] * rows[k * COMBINE_TILE:(k + 1) * COMBINE_TILE, :]
    o_ref[...] = _rms(y_ref[...] + moe, g_ref[...])


def _combine(pos, y, gates, g, ys, first_row, n_rows, name):
    off = first_row // COMBINE_TILE
    n_tiles = n_rows // COMBINE_TILE
    tok = lambda width: pl.BlockSpec((COMBINE_TILE, width), lambda i: (off + i, 0))
    idx = lambda step: pl.BlockSpec(
        (1, 1, TOP_K * COMBINE_TILE), lambda i: (off + jnp.minimum(i + step, n_tiles - 1), 0, 0),
        memory_space=pltpu.SMEM)
    pos = pos.reshape(N_TOK // COMBINE_TILE, 1, TOP_K * COMBINE_TILE)
    return pl.pallas_call(
        _combine_kernel,
        grid=(n_tiles,),
        in_specs=[idx(0), idx(1), tok(D_MODEL), tok(LANES),
                  pl.BlockSpec((1, D_MODEL), lambda i: (0, 0)),
                  pl.BlockSpec(memory_space=pl.ANY)],
        out_specs=pl.BlockSpec((COMBINE_TILE, D_MODEL), lambda i: (i, 0)),
        out_shape=jax.ShapeDtypeStruct((n_rows, D_MODEL), F32),
        scratch_shapes=[pltpu.VMEM((2, TOP_K * COMBINE_TILE, D_MODEL), F32), pltpu.SemaphoreType.DMA((2,))],
        compiler_params=_params("arbitrary"),
        name=name,
    )(pos, pos, y, gates, g, ys)


def kernel(x_prompt, x_sample, cache_attn_k, cache_attn_v, state_conv, norm_attn, w_in, conv_w,
           rel_table, norm_att_out, norm_conv_out, w_out, norm_ffn, w_router, b_router,
           w_gate_up, b_gate_up, w_down, b_down, norm_final):
    row = lambda g: g.reshape(1, -1)
    x_p = x_prompt.reshape(N_PROMPT, D_MODEL)
    x_s = x_sample.reshape(N_SAMPLE, D_MODEL)

    w_in_bf = w_in[0].astype(BF16)
    p_p = _norm_matmul(x_p, row(norm_attn[0]), w_in_bf, 1024, "norm_in_proj_prompt")
    p_s = _norm_matmul(x_s, row(norm_attn[0]), w_in_bf, 1024, "norm_in_proj_sample")

    r = cache_attn_k.shape[2]
    att_p = _attn_prompt(p_p, _prompt_bias(rel_table[0]))
    bias_c, bias_n = _sample_bias(rel_table[0], r)
    att_s = _attn_sample(p_s, cache_attn_k.reshape(DEC_BATCH, r, MIX_ATT),
                         cache_attn_v.reshape(DEC_BATCH, r, MIX_ATT), bias_c, bias_n)

    st = state_conv[0]
    pad = lambda a: jnp.pad(a, ((0, 0), (0, DEC_SEQ - a.shape[1]), (0, 0))).reshape(N_SAMPLE, MIX_CONV)
    merged, z = _mix(att_p, p_p, att_s, p_s, pad(st[:, 1:2]), pad(st), conv_w[0],
                     row(norm_att_out[0]), row(norm_conv_out[0]))
    y1 = _matmul_residual(merged, w_out[0].astype(BF16), x_p, x_s, 1024)

    w_r = jnp.pad(w_router[0], ((0, 0), (0, LANES - N_EXPERTS))).astype(BF16)
    b_r = jnp.pad(b_router[0], (0, LANES - N_EXPERTS)).reshape(1, LANES)
    h, idx, gates = _router(y1, row(norm_ffn[0]), w_r, b_r)

    pos, src, n_tiles, tile_start, tile_rows = _routing(idx[:, :TOP_K])
    xs = _gather(tile_rows, src, h)
    hs = _expert_matmul(n_tiles, tile_start, xs, w_gate_up.reshape(N_EXPERTS, D_MODEL, 2 * D_FF),
                        b_gate_up.reshape(N_EXPERTS, 1, 2 * D_FF),
                        parts=2, epilogue=_swiglu, out_dtype=BF16, name="expert_gate_up")
    ys = _expert_matmul(n_tiles, tile_start, hs, w_down.reshape(N_EXPERTS, D_FF, D_MODEL),
                        b_down.reshape(N_EXPERTS, 1, D_MODEL),
                        parts=1, epilogue=lambda a: a, out_dtype=F32, name="expert_down")
    y_p = _combine(pos, y1, gates, row(norm_final), ys, 0, N_PROMPT, "combine_final_norm_prompt")
    y_s = _combine(pos, y1, gates, row(norm_final), ys, N_PROMPT, N_SAMPLE, "combine_final_norm_sample")

    keep = min(BAND_ROWS, SEQ)
    shape_p = (1, 1, keep, N_HEADS, HEAD_DIM)
    shape_s = (1, DEC_BATCH, DEC_SEQ, N_HEADS, HEAD_DIM)
    z_s = z[N_PROMPT:].reshape(DEC_BATCH, DEC_SEQ, MIX_CONV)
    return (y_p.reshape(1, SEQ, D_MODEL),
            y_s.reshape(DEC_BATCH, DEC_SEQ, D_MODEL),
            p_p[N_PROMPT - keep:, MIX_ATT:2 * MIX_ATT].reshape(shape_p),
            p_p[N_PROMPT - keep:, 2 * MIX_ATT:3 * MIX_ATT].reshape(shape_p),
            z[N_PROMPT - 2:N_PROMPT].reshape(1, 1, 2, MIX_CONV),
            p_s[:, MIX_ATT:2 * MIX_ATT].reshape(shape_s),
            p_s[:, 2 * MIX_ATT:3 * MIX_ATT].reshape(shape_s),
            z_s[:, DEC_SEQ - 2:].reshape(1, DEC_BATCH, 2, MIX_CONV))
```

```python
import functools

import jax
import jax.numpy as jnp
import numpy as np
from jax import lax
from jax.experimental import pallas as pl
from jax.experimental.pallas import tpu as pltpu

D_MODEL = 4096
SEQ = 16384
DEC_BATCH = 32
DEC_SEQ = 16
CHUNK = 64
LEFT_CHUNKS = 8
BAND_ROWS = LEFT_CHUNKS * CHUNK
MIX_ATT = 2048
MIX_CONV = 2048
HEAD_DIM = 128
N_HEADS = 16
MAX_REL = 4 * CHUNK
IN_COLS = 3 * MIX_ATT + 3 * MIX_CONV
N_EXPERTS = 32
TOP_K = 4
D_FF = D_MODEL
SWIGLU_LIMIT = 7.0
SWIGLU_ALPHA = 1.702
EPS = 1e-5
SCALE = HEAD_DIM ** -0.5

N_PROMPT = SEQ
N_SAMPLE = DEC_BATCH * DEC_SEQ
N_TOK = N_PROMPT + N_SAMPLE
N_PAIRS = N_TOK * TOP_K

LANES = 128
ROW_TILE = 512
N_ROW_TILES = N_TOK // ROW_TILE
N_PROMPT_TILES = N_PROMPT // ROW_TILE
Q_ROWS = 2 * CHUNK
BAND_KEYS = BAND_ROWS + Q_ROWS
HALO_ROWS = 8
MIX_TILE = 128
MIX_PROMPT_TILES = N_PROMPT // MIX_TILE

MOE_TILE = 272
MOE_TILES = -(-N_PAIRS // MOE_TILE) + N_EXPERTS
MOE_ROWS = MOE_TILES * MOE_TILE
EXPERT_COLS = 2048
WEIGHT_CHUNK = 512
STAGE_SLOTS = 2
DMA_UNROLL = 8
COMBINE_TILE = 128

VMEM_LIMIT = 56 * 1024 * 1024

BF16 = jnp.bfloat16
F32 = jnp.float32
NEG_INF = float("-inf")


def _params(*sem):
    return pltpu.CompilerParams(dimension_semantics=sem, vmem_limit_bytes=VMEM_LIMIT)


def _rms(x, g):
    return x * lax.rsqrt(jnp.mean(x * x, axis=-1, keepdims=True) + EPS) * g


def _norm_matmul_kernel(x_ref, g_ref, w_ref, o_ref, h_ref):
    @pl.when(pl.program_id(1) == 0)
    def _():
        h_ref[...] = _rms(x_ref[...], g_ref[...]).astype(BF16)

    o_ref[...] = jnp.dot(h_ref[...], w_ref[...], preferred_element_type=F32)


def _norm_matmul(x, g, w, tn, name):
    m, k = x.shape
    n = w.shape[1]
    return pl.pallas_call(
        _norm_matmul_kernel,
        grid=(m // ROW_TILE, n // tn),
        in_specs=[
            pl.BlockSpec((ROW_TILE, k), lambda i, j: (i, 0)),
            pl.BlockSpec((1, k), lambda i, j: (0, 0)),
            pl.BlockSpec((k, tn), lambda i, j: (0, j)),
        ],
        out_specs=pl.BlockSpec((ROW_TILE, tn), lambda i, j: (i, j)),
        out_shape=jax.ShapeDtypeStruct((m, n), F32),
        scratch_shapes=[pltpu.VMEM((ROW_TILE, k), BF16)],
        compiler_params=_params("parallel", "arbitrary"),
        name=name,
    )(x, g, w)


def _attn_prompt_kernel(q_ref, kp_ref, kc_ref, vp_ref, vc_ref, b_ref, o_ref):
    m = pl.program_id(1)
    k = jnp.concatenate([kp_ref[...], kc_ref[...]], axis=0).astype(BF16)
    v = jnp.concatenate([vp_ref[...], vc_ref[...]], axis=0).astype(BF16)
    col = lax.broadcasted_iota(jnp.int32, (Q_ROWS, BAND_KEYS), 1)
    starts = range(0, ROW_TILE, Q_ROWS)
    scores = []
    for lo in starts:
        q = q_ref[lo:lo + Q_ROWS, :].astype(BF16)
        s = lax.dot_general(q, k[lo:lo + BAND_KEYS], (((1,), (1,)), ((), ())), preferred_element_type=F32)
        s = s * SCALE + b_ref[0]
        scores.append(jnp.where(col < jnp.where(m == 0, ROW_TILE - lo, 0), NEG_INF, s))
    probs = []
    for s in scores:
        e = jnp.exp(s - jnp.max(s, axis=-1, keepdims=True))
        probs.append((e / jnp.sum(e, axis=-1, keepdims=True)).astype(BF16))
    for lo, p in zip(starts, probs):
        o_ref[lo:lo + Q_ROWS, :] = jnp.dot(p, v[lo:lo + BAND_KEYS], preferred_element_type=F32)


def _attn_prompt(p_all, bias):
    blk = (ROW_TILE, HEAD_DIM)
    prev = lambda h, m: jnp.maximum(m - 1, 0)
    return pl.pallas_call(
        _attn_prompt_kernel,
        grid=(N_HEADS, N_PROMPT_TILES),
        in_specs=[
            pl.BlockSpec(blk, lambda h, m: (m, h)),
            pl.BlockSpec(blk, lambda h, m: (prev(h, m), N_HEADS + h)),
            pl.BlockSpec(blk, lambda h, m: (m, N_HEADS + h)),
            pl.BlockSpec(blk, lambda h, m: (prev(h, m), 2 * N_HEADS + h)),
            pl.BlockSpec(blk, lambda h, m: (m, 2 * N_HEADS + h)),
            pl.BlockSpec((1, Q_ROWS, BAND_KEYS), lambda h, m: (h, 0, 0)),
        ],
        out_specs=pl.BlockSpec(blk, lambda h, m: (m, h)),
        out_shape=jax.ShapeDtypeStruct((N_PROMPT, MIX_ATT), F32),
        compiler_params=_params("parallel", "arbitrary"),
        name="attn_prompt",
    )(p_all, p_all, p_all, p_all, p_all, bias)


def _prompt_bias(table):
    n_lags = Q_ROWS + BAND_KEYS - 1
    rel = BAND_ROWS + (Q_ROWS - 1 - jnp.arange(n_lags))
    per_lag = table[:, jnp.clip(rel, -MAX_REL, MAX_REL) + MAX_REL]
    period = jnp.pad(per_lag, ((0, 0), (0, 1)))
    skew = jnp.tile(period, (1, Q_ROWS))[:, :Q_ROWS * n_lags].reshape(N_HEADS, Q_ROWS, n_lags)
    toeplitz = skew[:, :, Q_ROWS - 1:Q_ROWS - 1 + BAND_KEYS]
    i = np.arange(Q_ROWS)[:, None]
    j = np.arange(BAND_KEYS)[None, :]
    chunk_lag = (LEFT_CHUNKS + i // CHUNK) - j // CHUNK
    band = (chunk_lag >= 0) & (chunk_lag <= LEFT_CHUNKS)
    return jnp.where(band[None], toeplitz, NEG_INF).astype(F32)


def _attn_sample_kernel(q_ref, k_ref, v_ref, ck_ref, cv_ref, bc_ref, bn_ref, o_ref):
    tb = (((1,), (1,)), ((), ()))
    heads = [slice(h * HEAD_DIM, (h + 1) * HEAD_DIM) for h in range(N_HEADS)]

    def head_rows(cache_ref, h):
        return cache_ref[pl.ds(h, cache_ref.shape[0] // N_HEADS, stride=N_HEADS), :]

    scores = []
    for h, sl in enumerate(heads):
        q = q_ref[:, sl].astype(BF16)
        sc = lax.dot_general(q, head_rows(ck_ref, h).astype(BF16), tb, preferred_element_type=F32)
        sn = lax.dot_general(q, k_ref[:, sl].astype(BF16), tb, preferred_element_type=F32)
        scores.append((sc * SCALE + bc_ref[h], sn * SCALE + bn_ref[h]))
    probs = []
    for sc, sn in scores:
        mx = jnp.maximum(jnp.max(sc, axis=-1, keepdims=True), jnp.max(sn, axis=-1, keepdims=True))
        ec = jnp.exp(sc - mx)
        en = jnp.exp(sn - mx)
        tot = jnp.sum(ec, axis=-1, keepdims=True) + jnp.sum(en, axis=-1, keepdims=True)
        probs.append(((ec / tot).astype(BF16), (en / tot).astype(BF16)))
    for h, (sl, (pc, pn)) in enumerate(zip(heads, probs)):
        o_ref[:, sl] = (jnp.dot(pc, head_rows(cv_ref, h).astype(BF16), preferred_element_type=F32)
                        + jnp.dot(pn, v_ref[:, sl].astype(BF16), preferred_element_type=F32))


def _attn_sample(p_s, cache_k, cache_v, bias_c, bias_n):
    r = cache_k.shape[1] // N_HEADS
    new = lambda c: pl.BlockSpec((DEC_SEQ, MIX_ATT), lambda b: (b, c))
    cache = pl.BlockSpec((None, r * N_HEADS, HEAD_DIM), lambda b: (b, 0, 0))
    return pl.pallas_call(
        _attn_sample_kernel,
        grid=(DEC_BATCH,),
        in_specs=[new(0), new(1), new(2), cache, cache,
                  pl.BlockSpec((N_HEADS, DEC_SEQ, r), lambda b: (0, 0, 0)),
                  pl.BlockSpec((N_HEADS, DEC_SEQ, DEC_SEQ), lambda b: (0, 0, 0))],
        out_specs=pl.BlockSpec((DEC_SEQ, MIX_ATT), lambda b: (b, 0)),
        out_shape=jax.ShapeDtypeStruct((N_SAMPLE, MIX_ATT), F32),
        compiler_params=_params("parallel"),
        name="attn_sample",
    )(p_s, p_s, p_s, cache_k, cache_v, bias_c, bias_n)


def _sample_bias(table, r):
    rel = (r + jnp.arange(DEC_SEQ))[:, None] - jnp.arange(r + DEC_SEQ)[None, :]
    bias = table[:, jnp.clip(rel, -MAX_REL, MAX_REL) + MAX_REL].astype(F32)
    return bias[:, :, :r], bias[:, :, r:]


def _mix_kernel(ap_ref, bgp_ref, cgp_ref, up_ref, cgh_ref, uh_ref, as_ref, bgs_ref, cgs_ref, us_ref,
                s1_ref, s2_ref, cw_ref, ga_ref, gc_ref, o_ref, z_ref):
    i = pl.program_id(0)
    sample = i >= MIX_PROMPT_TILES
    z = jnp.where(sample, cgs_ref[...] * us_ref[...], cgp_ref[...] * up_ref[...])
    z_ref[...] = z
    zh = jnp.where(i > 0, cgh_ref[...] * uh_ref[...], 0.0)
    row = lax.broadcasted_iota(jnp.int32, z.shape, 0)
    pos = jnp.where(sample, row & (DEC_SEQ - 1), row)
    z1 = jnp.where(pos == 0, jnp.where(sample, s1_ref[...], zh[7:8]), pltpu.roll(z, 1, 0))
    z2 = jnp.where(pos < 2,
                   jnp.where(sample, s2_ref[...], jnp.where(row == 0, zh[6:7], zh[7:8])),
                   pltpu.roll(z, 2, 0))
    y = cw_ref[0:1] * z2
    y = y + cw_ref[1:2] * z1
    y = y + cw_ref[2:3] * z
    att = jnp.where(sample, as_ref[...], ap_ref[...])
    bg = jnp.where(sample, bgs_ref[...], bgp_ref[...])
    o_ref[:, :MIX_ATT] = _rms(att, ga_ref[...]).astype(BF16)
    o_ref[:, MIX_ATT:] = _rms(bg * y, gc_ref[...]).astype(BF16)


def _mix(att_p, p_p, att_s, p_s, s1, s2, conv_w, g_att, g_conv):
    prompt = lambda c: pl.BlockSpec((MIX_TILE, MIX_CONV), lambda i: (jnp.minimum(i, MIX_PROMPT_TILES - 1), c))
    sample = lambda c: pl.BlockSpec((MIX_TILE, MIX_CONV), lambda i: (jnp.maximum(i - MIX_PROMPT_TILES, 0), c))
    halo = lambda c: pl.BlockSpec(
        (HALO_ROWS, MIX_CONV),
        lambda i: (jnp.clip(i * (MIX_TILE // HALO_ROWS) - 1, 0, N_PROMPT // HALO_ROWS - 1), c))
    const = lambda shape: pl.BlockSpec(shape, lambda i: (0, 0))
    return pl.pallas_call(
        _mix_kernel,
        grid=(N_TOK // MIX_TILE,),
        in_specs=[
            prompt(0), prompt(3), prompt(4), prompt(5), halo(4), halo(5),
            sample(0), sample(3), sample(4), sample(5),
            sample(0), sample(0),
            const((3, MIX_CONV)), const((1, MIX_ATT)), const((1, MIX_CONV)),
        ],
        out_specs=[pl.BlockSpec((MIX_TILE, D_MODEL), lambda i: (i, 0)),
                   pl.BlockSpec((MIX_TILE, MIX_CONV), lambda i: (i, 0))],
        out_shape=[jax.ShapeDtypeStruct((N_TOK, D_MODEL), BF16),
                   jax.ShapeDtypeStruct((N_TOK, MIX_CONV), F32)],
        compiler_params=_params("parallel"),
        name="conv_norms",
    )(att_p, p_p, p_p, p_p, p_p, p_p, att_s, p_s, p_s, p_s, s1, s2, conv_w, g_att, g_conv)


def _matmul_residual_kernel(a_ref, w_ref, rp_ref, rs_ref, o_ref):
    res = jnp.where(pl.program_id(0) < N_PROMPT_TILES, rp_ref[...], rs_ref[...])
    o_ref[...] = res + jnp.dot(a_ref[...], w_ref[...], preferred_element_type=F32)


def _matmul_residual(a, w, res_p, res_s, tn):
    m, k = a.shape
    n = w.shape[1]
    return pl.pallas_call(
        _matmul_residual_kernel,
        grid=(m // ROW_TILE, n // tn),
        in_specs=[
            pl.BlockSpec((ROW_TILE, k), lambda i, j: (i, 0)),
            pl.BlockSpec((k, tn), lambda i, j: (0, j)),
            pl.BlockSpec((ROW_TILE, tn), lambda i, j: (jnp.minimum(i, N_PROMPT_TILES - 1), j)),
            pl.BlockSpec((ROW_TILE, tn), lambda i, j: (0, j)),
        ],
        out_specs=pl.BlockSpec((ROW_TILE, tn), lambda i, j: (i, j)),
        out_shape=jax.ShapeDtypeStruct((m, n), F32),
        compiler_params=_params("parallel", "arbitrary"),
        name="out_proj",
    )(a, w, res_p, res_s)


def _router_kernel(y_ref, g_ref, w_ref, b_ref, h_ref, idx_ref, gate_ref):
    h = _rms(y_ref[...], g_ref[...])
    h_ref[...] = h
    logits = jnp.dot(h.astype(BF16), w_ref[...], preferred_element_type=F32) + b_ref[...]
    lane = lax.broadcasted_iota(jnp.int32, logits.shape, 1).astype(F32)
    left = jnp.where(lane < N_EXPERTS, logits, NEG_INF)
    vals, idxs = [], []
    for _ in range(TOP_K):
        top = jnp.max(left, axis=-1, keepdims=True)
        pick = jnp.min(jnp.where(left == top, lane, float(LANES)), axis=-1, keepdims=True)
        vals.append(top)
        idxs.append(pick)
        left = jnp.where(lane == pick, NEG_INF, left)
    exps = [jnp.exp(v - vals[0]) for v in vals]
    tot = exps[0]
    for e in exps[1:]:
        tot = tot + e
    idx_out = jnp.zeros_like(logits)
    gate_out = jnp.zeros_like(logits)
    for k in range(TOP_K):
        idx_out = jnp.where(lane == k, idxs[k], idx_out)
        gate_out = jnp.where(lane == k, exps[k] / tot, gate_out)
    idx_ref[...] = idx_out.astype(jnp.int32)
    gate_ref[...] = gate_out


def _router(y, g, w, b):
    const = lambda shape: pl.BlockSpec(shape, lambda i: (0, 0))
    return pl.pallas_call(
        _router_kernel,
        grid=(N_ROW_TILES,),
        in_specs=[pl.BlockSpec((ROW_TILE, D_MODEL), lambda i: (i, 0)),
                  const((1, D_MODEL)), const((D_MODEL, LANES)), const((1, LANES))],
        out_specs=[pl.BlockSpec((ROW_TILE, D_MODEL), lambda i: (i, 0)),
                   pl.BlockSpec((ROW_TILE, LANES), lambda i: (i, 0)),
                   pl.BlockSpec((ROW_TILE, LANES), lambda i: (i, 0))],
        out_shape=[jax.ShapeDtypeStruct((N_TOK, D_MODEL), F32),
                   jax.ShapeDtypeStruct((N_TOK, LANES), jnp.int32),
                   jax.ShapeDtypeStruct((N_TOK, LANES), F32)],
        compiler_params=_params("parallel"),
        name="ffn_norm_router",
    )(y, g, w, b)


def _routing(top_idx):
    e_flat = top_idx.reshape(-1)
    onehot = (e_flat[:, None] == jnp.arange(N_EXPERTS)[None, :]).astype(jnp.int32)
    csum = jnp.cumsum(onehot, axis=0)
    rank = jnp.take_along_axis(csum, e_flat[:, None], axis=1)[:, 0] - 1
    count = csum[-1]
    n_tiles = (count + MOE_TILE - 1) // MOE_TILE
    tile_end = jnp.cumsum(n_tiles)
    tile_start = tile_end - n_tiles
    pos = tile_start[e_flat] * MOE_TILE + rank
    src = jnp.zeros((MOE_ROWS,), jnp.int32).at[pos].set(
        jnp.arange(N_PAIRS, dtype=jnp.int32) // TOP_K, unique_indices=True, mode="promise_in_bounds")
    tile = jnp.arange(MOE_TILES, dtype=jnp.int32)
    owner = jnp.minimum(jnp.sum(tile_end[None, :] <= tile[:, None], axis=1), N_EXPERTS - 1)
    tile_rows = jnp.clip(count[owner] - (tile - tile_start[owner]) * MOE_TILE, 0, MOE_TILE)
    i32 = lambda a: a.astype(jnp.int32)
    return i32(pos), src, i32(n_tiles), i32(tile_start), i32(tile_rows)


def _row_copy(src_hbm, row, buf, slot, sem):
    return pltpu.make_async_copy(src_hbm.at[pl.ds(row, 1), :], buf.at[pl.ds(slot, 1), :], sem)


def _gather_kernel(rows_ref, src_ref, src_next_ref, h_hbm, o_ref, buf, sem):
    i = pl.program_id(0)
    slot = lax.rem(i, 2)

    def groups(tile):
        return (rows_ref[tile] + DMA_UNROLL - 1) // DMA_UNROLL

    def request(tile, idx_ref, dst):
        def start(g, c):
            for u in range(DMA_UNROLL):
                r = g * DMA_UNROLL + u
                _row_copy(h_hbm, idx_ref[0, 0, r], buf.at[dst], r, sem.at[dst]).start()
            return c

        lax.fori_loop(0, groups(tile), start, 0)

    @pl.when(i == 0)
    def _():
        buf[...] = jnp.zeros_like(buf)
        request(0, src_ref, 0)

    @pl.when(i + 1 < pl.num_programs(0))
    def _():
        request(i + 1, src_next_ref, 1 - slot)

    def wait(g, c):
        for u in range(DMA_UNROLL):
            _row_copy(h_hbm, 0, buf.at[slot], g * DMA_UNROLL + u, sem.at[slot]).wait()
        return c

    lax.fori_loop(0, groups(i), wait, 0)
    o_ref[...] = buf[slot].astype(BF16)


def _gather(tile_rows, src, h):
    src = src.reshape(MOE_TILES, 1, MOE_TILE)
    idx = lambda step: pl.BlockSpec(
        (1, 1, MOE_TILE), lambda i, rows: (jnp.minimum(i + step, MOE_TILES - 1), 0, 0), memory_space=pltpu.SMEM)
    return pl.pallas_call(
        _gather_kernel,
        grid_spec=pltpu.PrefetchScalarGridSpec(
            num_scalar_prefetch=1,
            grid=(MOE_TILES,),
            in_specs=[idx(0), idx(1), pl.BlockSpec(memory_space=pl.ANY)],
            out_specs=pl.BlockSpec((MOE_TILE, D_MODEL), lambda i, rows: (i, 0)),
            scratch_shapes=[pltpu.VMEM((2, MOE_TILE, D_MODEL), F32), pltpu.SemaphoreType.DMA((2,))],
        ),
        out_shape=jax.ShapeDtypeStruct((MOE_ROWS, D_MODEL), BF16),
        compiler_params=_params("arbitrary"),
        name="gather_rows",
    )(tile_rows, src, src, h)


def _swiglu(gate, up):
    gate = jnp.minimum(gate, SWIGLU_LIMIT)
    up = jnp.clip(up, -SWIGLU_LIMIT, SWIGLU_LIMIT)
    return (up + 1.0) * (gate * jax.nn.sigmoid(SWIGLU_ALPHA * gate))


def _expert_matmul_kernel(nt_ref, ts_ref, x_hbm, w_hbm, *refs, parts, part_stride, epilogue):
    b_refs = refs[:parts]
    o_hbm, xbuf, obuf, wbf, stage, xsem, osem, ssem, state = refs[parts:]
    tm, tn, ch = xbuf.shape[1], wbf.shape[3], stage.shape[2]
    n_chunks = wbf.shape[2] // ch
    n_col, n_exp = pl.num_programs(0), pl.num_programs(1)
    j, e = pl.program_id(0), pl.program_id(1)
    step = j * n_exp + e
    cur = lax.rem(step, 2)
    has_next = step + 1 < n_col * n_exp
    nxt_step = jnp.minimum(step + 1, n_col * n_exp - 1)
    j_nxt, e_nxt = nxt_step // n_exp, lax.rem(nxt_step, n_exp)
    n_rows, first = nt_ref[e], ts_ref[e]

    @pl.when(step == 0)
    def _():
        for s in range(3):
            state[s] = 0

    def chunk_copies(jj, ee, c):
        slot = lax.rem(c, STAGE_SLOTS)
        rows = pl.ds(pl.multiple_of(c * ch, ch), ch)
        return [pltpu.make_async_copy(
            w_hbm.at[ee, rows, pl.ds(pl.multiple_of(jj * tn + p * part_stride, tn), tn)],
            stage.at[slot, p], ssem.at[slot]) for p in range(parts)]

    def tile_load(tile, slot):
        return pltpu.make_async_copy(
            x_hbm.at[pl.ds(pl.multiple_of(tile * tm, tm), tm), :], xbuf.at[slot], xsem.at[slot])

    def x_copy(t):
        return tile_load(first + t, lax.rem(t, 2))

    def tile_store(tile, slot):
        return pltpu.make_async_copy(
            obuf.at[slot],
            o_hbm.at[pl.ds(pl.multiple_of(tile * tm, tm), tm), pl.ds(pl.multiple_of(j * tn, tn), tn)],
            osem.at[slot])

    tiles_done = state[2]

    def out_slot(t):
        return lax.rem(tiles_done + t, 2)

    def finish_stores():
        for s in range(2):
            @pl.when(state[s] == 1)
            def _():
                tile_store(0, s).wait()
                state[s] = 0

    def start_weights(jj, ee):
        for c in range(STAGE_SLOTS - 1):
            for cp in chunk_copies(jj, ee, c):
                cp.start()

    def chunk_arrive(jj, ee, c):
        @pl.when(c + STAGE_SLOTS - 1 < n_chunks)
        def _():
            for cp in chunk_copies(jj, ee, c + STAGE_SLOTS - 1):
                cp.start()

        for cp in chunk_copies(jj, ee, c):
            cp.wait()

    def chunk_convert(c, dst):
        slot = lax.rem(c, STAGE_SLOTS)
        rows = pl.ds(pl.multiple_of(c * ch, ch), ch)
        for p in range(parts):
            wbf[dst, p, rows, :] = stage[slot, p].astype(BF16)

    def rows_arrive(t):
        @pl.when(t + 1 < n_rows)
        def _():
            x_copy(t + 1).start()

        @pl.when(state[out_slot(t)] == 1)
        def _():
            tile_store(0, out_slot(t)).wait()

        x_copy(t).wait()

    def rows_compute(t, between=lambda: None):
        x = xbuf[lax.rem(t, 2)]
        slot = out_slot(t)
        if parts > 1:
            head = jnp.dot(x, wbf[cur, 0], preferred_element_type=F32) + b_refs[0][...]
            between()
            rest = [jnp.dot(x, wbf[cur, p], preferred_element_type=F32) + b_refs[p][...] for p in range(1, parts)]
            obuf[slot] = epilogue(head, *rest).astype(obuf.dtype)
        else:
            half = tn // 2
            head = jnp.dot(x, wbf[cur, 0, :, :half], preferred_element_type=F32) + b_refs[0][:, :half]
            between()
            obuf[slot, :, :half] = epilogue(head).astype(obuf.dtype)
            tail = jnp.dot(x, wbf[cur, 0, :, half:], preferred_element_type=F32) + b_refs[0][:, half:]
            obuf[slot, :, half:] = epilogue(tail).astype(obuf.dtype)
        tile_store(first + t, slot).start()
        state[slot] = 1

    @pl.when(step == 0)
    def _():
        start_weights(0, 0)

        def load(c, carry):
            chunk_arrive(0, 0, c)
            chunk_convert(c, 0)
            return carry

        lax.fori_loop(0, n_chunks, load, 0)

        @pl.when(has_next)
        def _():
            start_weights(j_nxt, e_nxt)

        @pl.when(n_rows > 0)
        def _():
            x_copy(0).start()

    todo = jnp.where(has_next, n_chunks, 0)

    def rows_and_chunk(t, carry):
        rows_arrive(t)

        @pl.when(t < todo)
        def _():
            chunk_arrive(j_nxt, e_nxt, t)

        rows_compute(t, lambda: chunk_convert(jnp.minimum(t, n_chunks - 1), 1 - cur))
        return carry

    def chunk_only(c, carry):
        chunk_arrive(j_nxt, e_nxt, c)
        chunk_convert(c, 1 - cur)
        return carry

    lax.fori_loop(0, n_rows, rows_and_chunk, 0)
    lax.fori_loop(jnp.minimum(n_rows, todo), todo, chunk_only, 0)

    state[2] = tiles_done + n_rows

    @pl.when(e == n_exp - 1)
    def _():
        finish_stores()
        used = first + n_rows
        total = o_hbm.shape[0] // tm
        obuf[0] = jnp.zeros(obuf.shape[1:], obuf.dtype)

        def start(tile, carry):
            tile_store(tile, 0).start()
            return carry

        def wait(tile, carry):
            tile_store(tile, 0).wait()
            return carry

        lax.fori_loop(used, total, start, 0)
        lax.fori_loop(used, total, wait, 0)

    after = step + 2

    @pl.when(after < n_col * n_exp)
    def _():
        start_weights(after // n_exp, lax.rem(after, n_exp))

    @pl.when(has_next & (nt_ref[e_nxt] > 0))
    def _():
        tile_load(ts_ref[e_nxt], 0).start()


def _expert_matmul(n_tiles, tile_start, x, w, b, *, parts, epilogue, out_dtype, name):
    n_exp, k, cols = w.shape
    n = cols // parts
    tn = EXPERT_COLS // parts
    n_col = n // tn
    bias = lambda p: pl.BlockSpec((None, 1, tn), lambda j, e, nt, ts: (e, 0, p * n_col + j))
    any_space = pl.BlockSpec(memory_space=pl.ANY)
    return pl.pallas_call(
        functools.partial(_expert_matmul_kernel, parts=parts, part_stride=n, epilogue=epilogue),
        grid_spec=pltpu.PrefetchScalarGridSpec(
            num_scalar_prefetch=2,
            grid=(n_col, n_exp),
            in_specs=[any_space, any_space] + [bias(p) for p in range(parts)],
            out_specs=any_space,
            scratch_shapes=[
                pltpu.VMEM((2, MOE_TILE, k), BF16),
                pltpu.VMEM((2, MOE_TILE, tn), out_dtype),
                pltpu.VMEM((2, parts, k, tn), BF16),
                pltpu.VMEM((STAGE_SLOTS, parts, WEIGHT_CHUNK, tn), F32),
                pltpu.SemaphoreType.DMA((2,)),
                pltpu.SemaphoreType.DMA((2,)),
                pltpu.SemaphoreType.DMA((STAGE_SLOTS,)),
                pltpu.SMEM((3,), jnp.int32),
            ],
        ),
        out_shape=jax.ShapeDtypeStruct((x.shape[0], n), out_dtype),
        compiler_params=_params("arbitrary", "arbitrary"),
        name=name,
    )(n_tiles, tile_start, x, w, *([b] * parts))


def _combine_kernel(pos_ref, pos_next_ref, y_ref, gate_ref, g_ref, ys_hbm, o_ref, buf, sem):
    i = pl.program_id(0)
    slot = lax.rem(i, 2)
    per_iter = DMA_UNROLL // TOP_K

    def request(idx_ref, dst):
        def start(it, c):
            for u in range(per_iter):
                t = it * per_iter + u
                for k in range(TOP_K):
                    _row_copy(ys_hbm, idx_ref[0, 0, t * TOP_K + k], buf.at[dst], k * COMBINE_TILE + t,
                              sem.at[dst]).start()
            return c

        lax.fori_loop(0, COMBINE_TILE // per_iter, start, 0)

    @pl.when(i == 0)
    def _():
        request(pos_ref, 0)

    @pl.when(i + 1 < pl.num_programs(0))
    def _():
        request(pos_next_ref, 1 - slot)

    def wait(it, c):
        for u in range(DMA_UNROLL):
            _row_copy(ys_hbm, 0, buf.at[slot], it * DMA_UNROLL + u, sem.at[slot]).wait()
        return c

    lax.fori_loop(0, TOP_K * COMBINE_TILE // DMA_UNROLL, wait, 0)
    gates = gate_ref[...]
    rows = buf.at[slot]
    moe = gates[:, 0:1] * rows[0:COMBINE_TILE, :]
    for k in range(1, TOP_K):
        moe = moe + gates[:, k:k + 1] * rows[k * COMBINE_TILE:(k + 1) * COMBINE_TILE, :]
    o_ref[...] = _rms(y_ref[...] + moe, g_ref[...])


def _combine(pos, y, gates, g, ys, first_row, n_rows, name):
    off = first_row // COMBINE_TILE
    n_tiles = n_rows // COMBINE_TILE
    tok = lambda width: pl.BlockSpec((COMBINE_TILE, width), lambda i: (off + i, 0))
    idx = lambda step: pl.BlockSpec(
        (1, 1, TOP_K * COMBINE_TILE), lambda i: (off + jnp.minimum(i + step, n_tiles - 1), 0, 0),
        memory_space=pltpu.SMEM)
    pos = pos.reshape(N_TOK // COMBINE_TILE, 1, TOP_K * COMBINE_TILE)
    return pl.pallas_call(
        _combine_kernel,
        grid=(n_tiles,),
        in_specs=[idx(0), idx(1), tok(D_MODEL), tok(LANES),
                  pl.BlockSpec((1, D_MODEL), lambda i: (0, 0)),
                  pl.BlockSpec(memory_space=pl.ANY)],
        out_specs=pl.BlockSpec((COMBINE_TILE, D_MODEL), lambda i: (i, 0)),
        out_shape=jax.ShapeDtypeStruct((n_rows, D_MODEL), F32),
        scratch_shapes=[pltpu.VMEM((2, TOP_K * COMBINE_TILE, D_MODEL), F32), pltpu.SemaphoreType.DMA((2,))],
        compiler_params=_params("arbitrary"),
        name=name,
    )(pos, pos, y, gates, g, ys)


def kernel(x_prompt, x_sample, cache_attn_k, cache_attn_v, state_conv, norm_attn, w_in, conv_w,
           rel_table, norm_att_out, norm_conv_out, w_out, norm_ffn, w_router, b_router,
           w_gate_up, b_gate_up, w_down, b_down, norm_final):
    row = lambda g: g.reshape(1, -1)
    x_p = x_prompt.reshape(N_PROMPT, D_MODEL)
    x_s = x_sample.reshape(N_SAMPLE, D_MODEL)

    w_in_bf = w_in[0].astype(BF16)
    p_p = _norm_matmul(x_p, row(norm_attn[0]), w_in_bf, 1024, "norm_in_proj_prompt")
    p_s = _norm_matmul(x_s, row(norm_attn[0]), w_in_bf, 1024, "norm_in_proj_sample")

    r = cache_attn_k.shape[2]
    att_p = _attn_prompt(p_p, _prompt_bias(rel_table[0]))
    bias_c, bias_n = _sample_bias(rel_table[0], r)
    att_s = _attn_sample(p_s, cache_attn_k.reshape(DEC_BATCH, r * N_HEADS, HEAD_DIM),
                         cache_attn_v.reshape(DEC_BATCH, r * N_HEADS, HEAD_DIM), bias_c, bias_n)

    st = state_conv[0]
    pad = lambda a: jnp.pad(a, ((0, 0), (0, DEC_SEQ - a.shape[1]), (0, 0))).reshape(N_SAMPLE, MIX_CONV)
    merged, z = _mix(att_p, p_p, att_s, p_s, pad(st[:, 1:2]), pad(st), conv_w[0],
                     row(norm_att_out[0]), row(norm_conv_out[0]))
    y1 = _matmul_residual(merged, w_out[0].astype(BF16), x_p, x_s, 1024)

    w_r = jnp.pad(w_router[0], ((0, 0), (0, LANES - N_EXPERTS))).astype(BF16)
    b_r = jnp.pad(b_router[0], (0, LANES - N_EXPERTS)).reshape(1, LANES)
    h, idx, gates = _router(y1, row(norm_ffn[0]), w_r, b_r)

    pos, src, n_tiles, tile_start, tile_rows = _routing(idx[:, :TOP_K])
    xs = _gather(tile_rows, src, h)
    hs = _expert_matmul(n_tiles, tile_start, xs, w_gate_up.reshape(N_EXPERTS, D_MODEL, 2 * D_FF),
                        b_gate_up.reshape(N_EXPERTS, 1, 2 * D_FF),
                        parts=2, epilogue=_swiglu, out_dtype=BF16, name="expert_gate_up")
    ys = _expert_matmul(n_tiles, tile_start, hs, w_down.reshape(N_EXPERTS, D_FF, D_MODEL),
                        b_down.reshape(N_EXPERTS, 1, D_MODEL),
                        parts=1, epilogue=lambda a: a, out_dtype=F32, name="expert_down")
    y_p = _combine(pos, y1, gates, row(norm_final), ys, 0, N_PROMPT, "combine_final_norm_prompt")
    y_s = _combine(pos, y1, gates, row(norm_final), ys, N_PROMPT, N_SAMPLE, "combine_final_norm_sample")

    keep = min(BAND_ROWS, SEQ)
    shape_p = (1, 1, keep, N_HEADS, HEAD_DIM)
    shape_s = (1, DEC_BATCH, DEC_SEQ, N_HEADS, HEAD_DIM)
    z_s = z[N_PROMPT:].reshape(DEC_BATCH, DEC_SEQ, MIX_CONV)
    return (y_p.reshape(1, SEQ, D_MODEL),
            y_s.reshape(DEC_BATCH, DEC_SEQ, D_MODEL),
            p_p[N_PROMPT - keep:, MIX_ATT:2 * MIX_ATT].reshape(shape_p),
            p_p[N_PROMPT - keep:, 2 * MIX_ATT:3 * MIX_ATT].reshape(shape_p),
            z[N_PROMPT - 2:N_PROMPT].reshape(1, 1, 2, MIX_CONV),
            p_s[:, MIX_ATT:2 * MIX_ATT].reshape(shape_s),
            p_s[:, 2 * MIX_ATT:3 * MIX_ATT].reshape(shape_s),
            z_s[:, DEC_SEQ - 2:].reshape(1, DEC_BATCH, 2, MIX_CONV))
```

```python
import functools

import jax
import jax.numpy as jnp
import numpy as np
from jax import lax
from jax.experimental import pallas as pl
from jax.experimental.pallas import tpu as pltpu

D_MODEL = 4096
SEQ = 16384
DEC_BATCH = 32
DEC_SEQ = 16
CHUNK = 64
LEFT_CHUNKS = 8
BAND_ROWS = LEFT_CHUNKS * CHUNK
MIX_ATT = 2048
MIX_CONV = 2048
HEAD_DIM = 128
N_HEADS = 16
MAX_REL = 4 * CHUNK
IN_COLS = 3 * MIX_ATT + 3 * MIX_CONV
N_EXPERTS = 32
TOP_K = 4
D_FF = D_MODEL
SWIGLU_LIMIT = 7.0
SWIGLU_ALPHA = 1.702
EPS = 1e-5
SCALE = HEAD_DIM ** -0.5

N_PROMPT = SEQ
N_SAMPLE = DEC_BATCH * DEC_SEQ
N_TOK = N_PROMPT + N_SAMPLE
N_PAIRS = N_TOK * TOP_K

LANES = 128
ROW_TILE = 512
N_ROW_TILES = N_TOK // ROW_TILE
N_PROMPT_TILES = N_PROMPT // ROW_TILE
Q_ROWS = 2 * CHUNK
BAND_KEYS = BAND_ROWS + Q_ROWS
HALO_ROWS = 8
MIX_TILE = 128
MIX_PROMPT_TILES = N_PROMPT // MIX_TILE

MOE_TILE = 272
MOE_TILES = -(-N_PAIRS // MOE_TILE) + N_EXPERTS
MOE_ROWS = MOE_TILES * MOE_TILE
EXPERT_COLS = 2048
WEIGHT_CHUNK = 512
STAGE_SLOTS = 2
DMA_UNROLL = 8
COMBINE_TILE = 128

VMEM_LIMIT = 56 * 1024 * 1024

BF16 = jnp.bfloat16
F32 = jnp.float32
NEG_INF = float("-inf")


def _params(*sem):
    return pltpu.CompilerParams(dimension_semantics=sem, vmem_limit_bytes=VMEM_LIMIT)


def _rms(x, g):
    return x * lax.rsqrt(jnp.mean(x * x, axis=-1, keepdims=True) + EPS) * g


def _norm_matmul_kernel(x_ref, g_ref, w_ref, o_ref, h_ref):
    @pl.when(pl.program_id(1) == 0)
    def _():
        h_ref[...] = _rms(x_ref[...], g_ref[...]).astype(BF16)

    o_ref[...] = jnp.dot(h_ref[...], w_ref[...], preferred_element_type=F32)


def _norm_matmul(x, g, w, tn, name):
    m, k = x.shape
    n = w.shape[1]
    return pl.pallas_call(
        _norm_matmul_kernel,
        grid=(m // ROW_TILE, n // tn),
        in_specs=[
            pl.BlockSpec((ROW_TILE, k), lambda i, j: (i, 0)),
            pl.BlockSpec((1, k), lambda i, j: (0, 0)),
            pl.BlockSpec((k, tn), lambda i, j: (0, j)),
        ],
        out_specs=pl.BlockSpec((ROW_TILE, tn), lambda i, j: (i, j)),
        out_shape=jax.ShapeDtypeStruct((m, n), F32),
        scratch_shapes=[pltpu.VMEM((ROW_TILE, k), BF16)],
        compiler_params=_params("parallel", "arbitrary"),
        name=name,
    )(x, g, w)


def _attn_prompt_kernel(q_ref, kp_ref, kc_ref, vp_ref, vc_ref, b_ref, o_ref):
    m = pl.program_id(1)
    k = jnp.concatenate([kp_ref[...], kc_ref[...]], axis=0).astype(BF16)
    v = jnp.concatenate([vp_ref[...], vc_ref[...]], axis=0).astype(BF16)
    col = lax.broadcasted_iota(jnp.int32, (Q_ROWS, BAND_KEYS), 1)
    starts = range(0, ROW_TILE, Q_ROWS)
    scores = []
    for lo in starts:
        q = q_ref[lo:lo + Q_ROWS, :].astype(BF16)
        s = lax.dot_general(q, k[lo:lo + BAND_KEYS], (((1,), (1,)), ((), ())), preferred_element_type=F32)
        s = s * SCALE + b_ref[0]
        scores.append(jnp.where(col < jnp.where(m == 0, ROW_TILE - lo, 0), NEG_INF, s))
    probs = []
    for s in scores:
        e = jnp.exp(s - jnp.max(s, axis=-1, keepdims=True))
        probs.append((e / jnp.sum(e, axis=-1, keepdims=True)).astype(BF16))
    for lo, p in zip(starts, probs):
        o_ref[lo:lo + Q_ROWS, :] = jnp.dot(p, v[lo:lo + BAND_KEYS], preferred_element_type=F32)


def _attn_prompt(p_all, bias):
    blk = (ROW_TILE, HEAD_DIM)
    prev = lambda h, m: jnp.maximum(m - 1, 0)
    return pl.pallas_call(
        _attn_prompt_kernel,
        grid=(N_HEADS, N_PROMPT_TILES),
        in_specs=[
            pl.BlockSpec(blk, lambda h, m: (m, h)),
            pl.BlockSpec(blk, lambda h, m: (prev(h, m), N_HEADS + h)),
            pl.BlockSpec(blk, lambda h, m: (m, N_HEADS + h)),
            pl.BlockSpec(blk, lambda h, m: (prev(h, m), 2 * N_HEADS + h)),
            pl.BlockSpec(blk, lambda h, m: (m, 2 * N_HEADS + h)),
            pl.BlockSpec((1, Q_ROWS, BAND_KEYS), lambda h, m: (h, 0, 0)),
        ],
        out_specs=pl.BlockSpec(blk, lambda h, m: (m, h)),
        out_shape=jax.ShapeDtypeStruct((N_PROMPT, MIX_ATT), F32),
        compiler_params=_params("parallel", "arbitrary"),
        name="attn_prompt",
    )(p_all, p_all, p_all, p_all, p_all, bias)


def _prompt_bias(table):
    n_lags = Q_ROWS + BAND_KEYS - 1
    rel = BAND_ROWS + (Q_ROWS - 1 - jnp.arange(n_lags))
    per_lag = table[:, jnp.clip(rel, -MAX_REL, MAX_REL) + MAX_REL]
    period = jnp.pad(per_lag, ((0, 0), (0, 1)))
    skew = jnp.tile(period, (1, Q_ROWS))[:, :Q_ROWS * n_lags].reshape(N_HEADS, Q_ROWS, n_lags)
    toeplitz = skew[:, :, Q_ROWS - 1:Q_ROWS - 1 + BAND_KEYS]
    i = np.arange(Q_ROWS)[:, None]
    j = np.arange(BAND_KEYS)[None, :]
    chunk_lag = (LEFT_CHUNKS + i // CHUNK) - j // CHUNK
    band = (chunk_lag >= 0) & (chunk_lag <= LEFT_CHUNKS)
    return jnp.where(band[None], toeplitz, NEG_INF).astype(F32)


def _attn_sample_kernel(q_ref, k_ref, v_ref, ck_ref, cv_ref, bc_ref, bn_ref, o_ref):
    tb = (((1,), (1,)), ((), ()))
    heads = [slice(h * HEAD_DIM, (h + 1) * HEAD_DIM) for h in range(N_HEADS)]

    def head_rows(cache_ref, h):
        return cache_ref[pl.ds(h, cache_ref.shape[0] // N_HEADS, stride=N_HEADS), :]

    scores = []
    for h, sl in enumerate(heads):
        q = q_ref[:, sl].astype(BF16)
        sc = lax.dot_general(q, head_rows(ck_ref, h).astype(BF16), tb, preferred_element_type=F32)
        sn = lax.dot_general(q, k_ref[:, sl].astype(BF16), tb, preferred_element_type=F32)
        scores.append((sc * SCALE + bc_ref[h], sn * SCALE + bn_ref[h]))
    probs = []
    for sc, sn in scores:
        mx = jnp.maximum(jnp.max(sc, axis=-1, keepdims=True), jnp.max(sn, axis=-1, keepdims=True))
        ec = jnp.exp(sc - mx)
        en = jnp.exp(sn - mx)
        tot = jnp.sum(ec, axis=-1, keepdims=True) + jnp.sum(en, axis=-1, keepdims=True)
        probs.append(((ec / tot).astype(BF16), (en / tot).astype(BF16)))
    for h, (sl, (pc, pn)) in enumerate(zip(heads, probs)):
        o_ref[:, sl] = (jnp.dot(pc, head_rows(cv_ref, h).astype(BF16), preferred_element_type=F32)
                        + jnp.dot(pn, v_ref[:, sl].astype(BF16), preferred_element_type=F32))


def _attn_sample(p_s, cache_k, cache_v, bias_c, bias_n):
    r = cache_k.shape[1] // N_HEADS
    new = lambda c: pl.BlockSpec((DEC_SEQ, MIX_ATT), lambda b: (b, c))
    cache = pl.BlockSpec((None, r * N_HEADS, HEAD_DIM), lambda b: (b, 0, 0))
    return pl.pallas_call(
        _attn_sample_kernel,
        grid=(DEC_BATCH,),
        in_specs=[new(0), new(1), new(2), cache, cache,
                  pl.BlockSpec((N_HEADS, DEC_SEQ, r), lambda b: (0, 0, 0)),
                  pl.BlockSpec((N_HEADS, DEC_SEQ, DEC_SEQ), lambda b: (0, 0, 0))],
        out_specs=pl.BlockSpec((DEC_SEQ, MIX_ATT), lambda b: (b, 0)),
        out_shape=jax.ShapeDtypeStruct((N_SAMPLE, MIX_ATT), F32),
        compiler_params=_params("parallel"),
        name="attn_sample",
    )(p_s, p_s, p_s, cache_k, cache_v, bias_c, bias_n)


def _sample_bias(table, r):
    rel = (r + jnp.arange(DEC_SEQ))[:, None] - jnp.arange(r + DEC_SEQ)[None, :]
    bias = table[:, jnp.clip(rel, -MAX_REL, MAX_REL) + MAX_REL].astype(F32)
    return bias[:, :, :r], bias[:, :, r:]


def _mix_kernel(ap_ref, bgp_ref, cgp_ref, up_ref, cgh_ref, uh_ref, as_ref, bgs_ref, cgs_ref, us_ref,
                s1_ref, s2_ref, cw_ref, ga_ref, gc_ref, o_ref, z_ref):
    i = pl.program_id(0)
    sample = i >= MIX_PROMPT_TILES
    z = jnp.where(sample, cgs_ref[...] * us_ref[...], cgp_ref[...] * up_ref[...])
    z_ref[...] = z
    zh = jnp.where(i > 0, cgh_ref[...] * uh_ref[...], 0.0)
    row = lax.broadcasted_iota(jnp.int32, z.shape, 0)
    pos = jnp.where(sample, row & (DEC_SEQ - 1), row)
    z1 = jnp.where(pos == 0, jnp.where(sample, s1_ref[...], zh[7:8]), pltpu.roll(z, 1, 0))
    z2 = jnp.where(pos < 2,
                   jnp.where(sample, s2_ref[...], jnp.where(row == 0, zh[6:7], zh[7:8])),
                   pltpu.roll(z, 2, 0))
    y = cw_ref[0:1] * z2
    y = y + cw_ref[1:2] * z1
    y = y + cw_ref[2:3] * z
    att = jnp.where(sample, as_ref[...], ap_ref[...])
    bg = jnp.where(sample, bgs_ref[...], bgp_ref[...])
    o_ref[:, :MIX_ATT] = _rms(att, ga_ref[...]).astype(BF16)
    o_ref[:, MIX_ATT:] = _rms(bg * y, gc_ref[...]).astype(BF16)


def _mix(att_p, p_p, att_s, p_s, s1, s2, conv_w, g_att, g_conv):
    prompt = lambda c: pl.BlockSpec((MIX_TILE, MIX_CONV), lambda i: (jnp.minimum(i, MIX_PROMPT_TILES - 1), c))
    sample = lambda c: pl.BlockSpec((MIX_TILE, MIX_CONV), lambda i: (jnp.maximum(i - MIX_PROMPT_TILES, 0), c))
    halo = lambda c: pl.BlockSpec(
        (HALO_ROWS, MIX_CONV),
        lambda i: (jnp.clip(i * (MIX_TILE // HALO_ROWS) - 1, 0, N_PROMPT // HALO_ROWS - 1), c))
    const = lambda shape: pl.BlockSpec(shape, lambda i: (0, 0))
    return pl.pallas_call(
        _mix_kernel,
        grid=(N_TOK // MIX_TILE,),
        in_specs=[
            prompt(0), prompt(3), prompt(4), prompt(5), halo(4), halo(5),
            sample(0), sample(3), sample(4), sample(5),
            sample(0), sample(0),
            const((3, MIX_CONV)), const((1, MIX_ATT)), const((1, MIX_CONV)),
        ],
        out_specs=[pl.BlockSpec((MIX_TILE, D_MODEL), lambda i: (i, 0)),
                   pl.BlockSpec((MIX_TILE, MIX_CONV), lambda i: (i, 0))],
        out_shape=[jax.ShapeDtypeStruct((N_TOK, D_MODEL), BF16),
                   jax.ShapeDtypeStruct((N_TOK, MIX_CONV), F32)],
        compiler_params=_params("parallel"),
        name="conv_norms",
    )(att_p, p_p, p_p, p_p, p_p, p_p, att_s, p_s, p_s, p_s, s1, s2, conv_w, g_att, g_conv)


def _matmul_residual_kernel(a_ref, w_ref, rp_ref, rs_ref, o_ref):
    res = jnp.where(pl.program_id(0) < N_PROMPT_TILES, rp_ref[...], rs_ref[...])
    o_ref[...] = res + jnp.dot(a_ref[...], w_ref[...], preferred_element_type=F32)


def _matmul_residual(a, w, res_p, res_s, tn):
    m, k = a.shape
    n = w.shape[1]
    return pl.pallas_call(
        _matmul_residual_kernel,
        grid=(m // ROW_TILE, n // tn),
        in_specs=[
            pl.BlockSpec((ROW_TILE, k), lambda i, j: (i, 0)),
            pl.BlockSpec((k, tn), lambda i, j: (0, j)),
            pl.BlockSpec((ROW_TILE, tn), lambda i, j: (jnp.minimum(i, N_PROMPT_TILES - 1), j)),
            pl.BlockSpec((ROW_TILE, tn), lambda i, j: (0, j)),
        ],
        out_specs=pl.BlockSpec((ROW_TILE, tn), lambda i, j: (i, j)),
        out_shape=jax.ShapeDtypeStruct((m, n), F32),
        compiler_params=_params("parallel", "arbitrary"),
        name="out_proj",
    )(a, w, res_p, res_s)


def _router_kernel(y_ref, g_ref, w_ref, b_ref, h_ref, idx_ref, gate_ref):
    h = _rms(y_ref[...], g_ref[...])
    h_ref[...] = h
    logits = jnp.dot(h.astype(BF16), w_ref[...], preferred_element_type=F32) + b_ref[...]
    lane = lax.broadcasted_iota(jnp.int32, logits.shape, 1).astype(F32)
    left = jnp.where(lane < N_EXPERTS, logits, NEG_INF)
    vals, idxs = [], []
    for _ in range(TOP_K):
        top = jnp.max(left, axis=-1, keepdims=True)
        pick = jnp.min(jnp.where(left == top, lane, float(LANES)), axis=-1, keepdims=True)
        vals.append(top)
        idxs.append(pick)
        left = jnp.where(lane == pick, NEG_INF, left)
    exps = [jnp.exp(v - vals[0]) for v in vals]
    tot = exps[0]
    for e in exps[1:]:
        tot = tot + e
    idx_out = jnp.zeros_like(logits)
    gate_out = jnp.zeros_like(logits)
    for k in range(TOP_K):
        idx_out = jnp.where(lane == k, idxs[k], idx_out)
        gate_out = jnp.where(lane == k, exps[k] / tot, gate_out)
    idx_ref[...] = idx_out.astype(jnp.int32)
    gate_ref[...] = gate_out


def _router(y, g, w, b):
    const = lambda shape: pl.BlockSpec(shape, lambda i: (0, 0))
    return pl.pallas_call(
        _router_kernel,
        grid=(N_ROW_TILES,),
        in_specs=[pl.BlockSpec((ROW_TILE, D_MODEL), lambda i: (i, 0)),
                  const((1, D_MODEL)), const((D_MODEL, LANES)), const((1, LANES))],
        out_specs=[pl.BlockSpec((ROW_TILE, D_MODEL), lambda i: (i, 0)),
                   pl.BlockSpec((ROW_TILE, LANES), lambda i: (i, 0)),
                   pl.BlockSpec((ROW_TILE, LANES), lambda i: (i, 0))],
        out_shape=[jax.ShapeDtypeStruct((N_TOK, D_MODEL), F32),
                   jax.ShapeDtypeStruct((N_TOK, LANES), jnp.int32),
                   jax.ShapeDtypeStruct((N_TOK, LANES), F32)],
        compiler_params=_params("parallel"),
        name="ffn_norm_router",
    )(y, g, w, b)


def _routing(top_idx):
    e_flat = top_idx.reshape(-1)
    onehot = (e_flat[:, None] == jnp.arange(N_EXPERTS)[None, :]).astype(jnp.int32)
    csum = jnp.cumsum(onehot, axis=0)
    rank = jnp.take_along_axis(csum, e_flat[:, None], axis=1)[:, 0] - 1
    count = csum[-1]
    n_tiles = (count + MOE_TILE - 1) // MOE_TILE
    tile_end = jnp.cumsum(n_tiles)
    tile_start = tile_end - n_tiles
    pos = tile_start[e_flat] * MOE_TILE + rank
    src = jnp.zeros((MOE_ROWS,), jnp.int32).at[pos].set(
        jnp.arange(N_PAIRS, dtype=jnp.int32) // TOP_K, unique_indices=True, mode="promise_in_bounds")
    tile = jnp.arange(MOE_TILES, dtype=jnp.int32)
    owner = jnp.minimum(jnp.sum(tile_end[None, :] <= tile[:, None], axis=1), N_EXPERTS - 1)
    tile_rows = jnp.clip(count[owner] - (tile - tile_start[owner]) * MOE_TILE, 0, MOE_TILE)
    i32 = lambda a: a.astype(jnp.int32)
    return i32(pos), src, i32(n_tiles), i32(tile_start), i32(tile_rows)


def _row_copy(src_hbm, row, buf, slot, sem):
    return pltpu.make_async_copy(src_hbm.at[pl.ds(row, 1), :], buf.at[pl.ds(slot, 1), :], sem)


def _gather_kernel(rows_ref, src_ref, src_next_ref, h_hbm, o_ref, buf, sem):
    i = pl.program_id(0)
    slot = lax.rem(i, 2)

    def groups(tile):
        return (rows_ref[tile] + DMA_UNROLL - 1) // DMA_UNROLL

    def request(tile, idx_ref, dst):
        def start(g, c):
            for u in range(DMA_UNROLL):
                r = g * DMA_UNROLL + u
                _row_copy(h_hbm, idx_ref[0, 0, r], buf.at[dst], r, sem.at[dst]).start(priority=u % 2)
            return c

        lax.fori_loop(0, groups(tile), start, 0)

    @pl.when(i == 0)
    def _():
        buf[...] = jnp.zeros_like(buf)
        request(0, src_ref, 0)

    @pl.when(i + 1 < pl.num_programs(0))
    def _():
        request(i + 1, src_next_ref, 1 - slot)

    def wait(g, c):
        for u in range(DMA_UNROLL):
            _row_copy(h_hbm, 0, buf.at[slot], g * DMA_UNROLL + u, sem.at[slot]).wait()
        return c

    lax.fori_loop(0, groups(i), wait, 0)
    o_ref[...] = buf[slot].astype(BF16)


def _gather(tile_rows, src, h):
    src = src.reshape(MOE_TILES, 1, MOE_TILE)
    idx = lambda step: pl.BlockSpec(
        (1, 1, MOE_TILE), lambda i, rows: (jnp.minimum(i + step, MOE_TILES - 1), 0, 0), memory_space=pltpu.SMEM)
    return pl.pallas_call(
        _gather_kernel,
        grid_spec=pltpu.PrefetchScalarGridSpec(
            num_scalar_prefetch=1,
            grid=(MOE_TILES,),
            in_specs=[idx(0), idx(1), pl.BlockSpec(memory_space=pl.ANY)],
            out_specs=pl.BlockSpec((MOE_TILE, D_MODEL), lambda i, rows: (i, 0)),
            scratch_shapes=[pltpu.VMEM((2, MOE_TILE, D_MODEL), F32), pltpu.SemaphoreType.DMA((2,))],
        ),
        out_shape=jax.ShapeDtypeStruct((MOE_ROWS, D_MODEL), BF16),
        compiler_params=_params("arbitrary"),
        name="gather_rows",
    )(tile_rows, src, src, h)


def _swiglu(gate, up):
    gate = jnp.minimum(gate, SWIGLU_LIMIT)
    up = jnp.clip(up, -SWIGLU_LIMIT, SWIGLU_LIMIT)
    return (up + 1.0) * (gate * jax.nn.sigmoid(SWIGLU_ALPHA * gate))


def _expert_matmul_kernel(nt_ref, ts_ref, x_hbm, w_hbm, *refs, parts, part_stride, epilogue):
    b_refs = refs[:parts]
    o_hbm, xbuf, obuf, wbf, stage, xsem, osem, ssem, state = refs[parts:]
    tm, tn, ch = xbuf.shape[1], wbf.shape[3], stage.shape[2]
    n_chunks = wbf.shape[2] // ch
    n_col, n_exp = pl.num_programs(0), pl.num_programs(1)
    j, e = pl.program_id(0), pl.program_id(1)
    step = j * n_exp + e
    cur = lax.rem(step, 2)
    has_next = step + 1 < n_col * n_exp
    nxt_step = jnp.minimum(step + 1, n_col * n_exp - 1)
    j_nxt, e_nxt = nxt_step // n_exp, lax.rem(nxt_step, n_exp)
    n_rows, first = nt_ref[e], ts_ref[e]

    @pl.when(step == 0)
    def _():
        for s in range(3):
            state[s] = 0

    def chunk_copies(jj, ee, c):
        slot = lax.rem(c, STAGE_SLOTS)
        rows = pl.ds(pl.multiple_of(c * ch, ch), ch)
        return [pltpu.make_async_copy(
            w_hbm.at[ee, rows, pl.ds(pl.multiple_of(jj * tn + p * part_stride, tn), tn)],
            stage.at[slot, p], ssem.at[slot]) for p in range(parts)]

    def tile_load(tile, slot):
        return pltpu.make_async_copy(
            x_hbm.at[pl.ds(pl.multiple_of(tile * tm, tm), tm), :], xbuf.at[slot], xsem.at[slot])

    def x_copy(t):
        return tile_load(first + t, lax.rem(t, 2))

    def tile_store(tile, slot):
        return pltpu.make_async_copy(
            obuf.at[slot],
            o_hbm.at[pl.ds(pl.multiple_of(tile * tm, tm), tm), pl.ds(pl.multiple_of(j * tn, tn), tn)],
            osem.at[slot])

    tiles_done = state[2]

    def out_slot(t):
        return lax.rem(tiles_done + t, 2)

    def finish_stores():
        for s in range(2):
            @pl.when(state[s] == 1)
            def _():
                tile_store(0, s).wait()
                state[s] = 0

    def start_weights(jj, ee):
        for c in range(STAGE_SLOTS - 1):
            for cp in chunk_copies(jj, ee, c):
                cp.start()

    def chunk_arrive(jj, ee, c):
        @pl.when(c + STAGE_SLOTS - 1 < n_chunks)
        def _():
            for cp in chunk_copies(jj, ee, c + STAGE_SLOTS - 1):
                cp.start()

        for cp in chunk_copies(jj, ee, c):
            cp.wait()

    def chunk_convert(c, dst):
        slot = lax.rem(c, STAGE_SLOTS)
        rows = pl.ds(pl.multiple_of(c * ch, ch), ch)
        for p in range(parts):
            wbf[dst, p, rows, :] = stage[slot, p].astype(BF16)

    def rows_arrive(t):
        @pl.when(t + 1 < n_rows)
        def _():
            x_copy(t + 1).start()

        @pl.when(state[out_slot(t)] == 1)
        def _():
            tile_store(0, out_slot(t)).wait()

        x_copy(t).wait()

    def rows_compute(t, between=lambda: None):
        x = xbuf[lax.rem(t, 2)]
        slot = out_slot(t)
        if parts > 1:
            head = jnp.dot(x, wbf[cur, 0], preferred_element_type=F32) + b_refs[0][...]
            between()
            rest = [jnp.dot(x, wbf[cur, p], preferred_element_type=F32) + b_refs[p][...] for p in range(1, parts)]
            obuf[slot] = epilogue(head, *rest).astype(obuf.dtype)
        else:
            half = tn // 2
            head = jnp.dot(x, wbf[cur, 0, :, :half], preferred_element_type=F32) + b_refs[0][:, :half]
            between()
            obuf[slot, :, :half] = epilogue(head).astype(obuf.dtype)
            tail = jnp.dot(x, wbf[cur, 0, :, half:], preferred_element_type=F32) + b_refs[0][:, half:]
            obuf[slot, :, half:] = epilogue(tail).astype(obuf.dtype)
        tile_store(first + t, slot).start()
        state[slot] = 1

    @pl.when(step == 0)
    def _():
        start_weights(0, 0)

        def load(c, carry):
            chunk_arrive(0, 0, c)
            chunk_convert(c, 0)
            return carry

        lax.fori_loop(0, n_chunks, load, 0)

        @pl.when(has_next)
        def _():
            start_weights(j_nxt, e_nxt)

        @pl.when(n_rows > 0)
        def _():
            x_copy(0).start()

    todo = jnp.where(has_next, n_chunks, 0)

    def rows_and_chunk(t, carry):
        rows_arrive(t)

        @pl.when(t < todo)
        def _():
            chunk_arrive(j_nxt, e_nxt, t)

        rows_compute(t, lambda: chunk_convert(jnp.minimum(t, n_chunks - 1), 1 - cur))
        return carry

    def chunk_only(c, carry):
        chunk_arrive(j_nxt, e_nxt, c)
        chunk_convert(c, 1 - cur)
        return carry

    lax.fori_loop(0, n_rows, rows_and_chunk, 0)
    lax.fori_loop(jnp.minimum(n_rows, todo), todo, chunk_only, 0)

    state[2] = tiles_done + n_rows

    @pl.when(e == n_exp - 1)
    def _():
        finish_stores()
        used = first + n_rows
        total = o_hbm.shape[0] // tm
        obuf[0] = jnp.zeros(obuf.shape[1:], obuf.dtype)

        def start(tile, carry):
            tile_store(tile, 0).start()
            return carry

        def wait(tile, carry):
            tile_store(tile, 0).wait()
            return carry

        lax.fori_loop(used, total, start, 0)
        lax.fori_loop(used, total, wait, 0)

    after = step + 2

    @pl.when(after < n_col * n_exp)
    def _():
        start_weights(after // n_exp, lax.rem(after, n_exp))

    @pl.when(has_next & (nt_ref[e_nxt] > 0))
    def _():
        tile_load(ts_ref[e_nxt], 0).start()


def _expert_matmul(n_tiles, tile_start, x, w, b, *, parts, epilogue, out_dtype, name):
    n_exp, k, cols = w.shape
    n = cols // parts
    tn = EXPERT_COLS // parts
    n_col = n // tn
    bias = lambda p: pl.BlockSpec((None, 1, tn), lambda j, e, nt, ts: (e, 0, p * n_col + j))
    any_space = pl.BlockSpec(memory_space=pl.ANY)
    return pl.pallas_call(
        functools.partial(_expert_matmul_kernel, parts=parts, part_stride=n, epilogue=epilogue),
        grid_spec=pltpu.PrefetchScalarGridSpec(
            num_scalar_prefetch=2,
            grid=(n_col, n_exp),
            in_specs=[any_space, any_space] + [bias(p) for p in range(parts)],
            out_specs=any_space,
            scratch_shapes=[
                pltpu.VMEM((2, MOE_TILE, k), BF16),
                pltpu.VMEM((2, MOE_TILE, tn), out_dtype),
                pltpu.VMEM((2, parts, k, tn), BF16),
                pltpu.VMEM((STAGE_SLOTS, parts, WEIGHT_CHUNK, tn), F32),
                pltpu.SemaphoreType.DMA((2,)),
                pltpu.SemaphoreType.DMA((2,)),
                pltpu.SemaphoreType.DMA((STAGE_SLOTS,)),
                pltpu.SMEM((3,), jnp.int32),
            ],
        ),
        out_shape=jax.ShapeDtypeStruct((x.shape[0], n), out_dtype),
        compiler_params=_params("arbitrary", "arbitrary"),
        name=name,
    )(n_tiles, tile_start, x, w, *([b] * parts))


def _combine_kernel(pos_ref, pos_next_ref, y_ref, gate_ref, g_ref, ys_hbm, o_ref, buf, sem):
    i = pl.program_id(0)
    slot = lax.rem(i, 2)
    per_iter = DMA_UNROLL // TOP_K

    def request(idx_ref, dst):
        def start(it, c):
            for u in range(per_iter):
                t = it * per_iter + u
                for k in range(TOP_K):
                    _row_copy(ys_hbm, idx_ref[0, 0, t * TOP_K + k], buf.at[dst], k * COMBINE_TILE + t,
                              sem.at[dst]).start(priority=k % 2)
            return c

        lax.fori_loop(0, COMBINE_TILE // per_iter, start, 0)

    @pl.when(i == 0)
    def _():
        request(pos_ref, 0)

    @pl.when(i + 1 < pl.num_programs(0))
    def _():
        request(pos_next_ref, 1 - slot)

    def wait(it, c):
        for u in range(DMA_UNROLL):
            _row_copy(ys_hbm, 0, buf.at[slot], it * DMA_UNROLL + u, sem.at[slot]).wait()
        return c

    lax.fori_loop(0, TOP_K * COMBINE_TILE // DMA_UNROLL, wait, 0)
    gates = gate_ref[...]
    rows = buf.at[slot]
    moe = gates[:, 0:1] * rows[0:COMBINE_TILE, :]
    for k in range(1, TOP_K):
        moe = moe + gates[:, k:k + 1] * rows[k * COMBINE_TILE:(k + 1) * COMBINE_TILE, :]
    o_ref[...] = _rms(y_ref[...] + moe, g_ref[...])


def _combine(pos, y, gates, g, ys, first_row, n_rows, name):
    off = first_row // COMBINE_TILE
    n_tiles = n_rows // COMBINE_TILE
    tok = lambda width: pl.BlockSpec((COMBINE_TILE, width), lambda i: (off + i, 0))
    idx = lambda step: pl.BlockSpec(
        (1, 1, TOP_K * COMBINE_TILE), lambda i: (off + jnp.minimum(i + step, n_tiles - 1), 0, 0),
        memory_space=pltpu.SMEM)
    pos = pos.reshape(N_TOK // COMBINE_TILE, 1, TOP_K * COMBINE_TILE)
    return pl.pallas_call(
        _combine_kernel,
        grid=(n_tiles,),
        in_specs=[idx(0), idx(1), tok(D_MODEL), tok(LANES),
                  pl.BlockSpec((1, D_MODEL), lambda i: (0, 0)),
                  pl.BlockSpec(memory_space=pl.ANY)],
        out_specs=pl.BlockSpec((COMBINE_TILE, D_MODEL), lambda i: (i, 0)),
        out_shape=jax.ShapeDtypeStruct((n_rows, D_MODEL), F32),
        scratch_shapes=[pltpu.VMEM((2, TOP_K * COMBINE_TILE, D_MODEL), F32), pltpu.SemaphoreType.DMA((2,))],
        compiler_params=_params("arbitrary"),
        name=name,
    )(pos, pos, y, gates, g, ys)


def kernel(x_prompt, x_sample, cache_attn_k, cache_attn_v, state_conv, norm_attn, w_in, conv_w,
           rel_table, norm_att_out, norm_conv_out, w_out, norm_ffn, w_router, b_router,
           w_gate_up, b_gate_up, w_down, b_down, norm_final):
    row = lambda g: g.reshape(1, -1)
    x_p = x_prompt.reshape(N_PROMPT, D_MODEL)
    x_s = x_sample.reshape(N_SAMPLE, D_MODEL)

    w_in_bf = w_in[0].astype(BF16)
    p_p = _norm_matmul(x_p, row(norm_attn[0]), w_in_bf, 1024, "norm_in_proj_prompt")
    p_s = _norm_matmul(x_s, row(norm_attn[0]), w_in_bf, 1024, "norm_in_proj_sample")

    r = cache_attn_k.shape[2]
    att_p = _attn_prompt(p_p, _prompt_bias(rel_table[0]))
    bias_c, bias_n = _sample_bias(rel_table[0], r)
    att_s = _attn_sample(p_s, cache_attn_k.reshape(DEC_BATCH, r * N_HEADS, HEAD_DIM),
                         cache_attn_v.reshape(DEC_BATCH, r * N_HEADS, HEAD_DIM), bias_c, bias_n)

    st = state_conv[0]
    pad = lambda a: jnp.pad(a, ((0, 0), (0, DEC_SEQ - a.shape[1]), (0, 0))).reshape(N_SAMPLE, MIX_CONV)
    merged, z = _mix(att_p, p_p, att_s, p_s, pad(st[:, 1:2]), pad(st), conv_w[0],
                     row(norm_att_out[0]), row(norm_conv_out[0]))
    y1 = _matmul_residual(merged, w_out[0].astype(BF16), x_p, x_s, 1024)

    w_r = jnp.pad(w_router[0], ((0, 0), (0, LANES - N_EXPERTS))).astype(BF16)
    b_r = jnp.pad(b_router[0], (0, LANES - N_EXPERTS)).reshape(1, LANES)
    h, idx, gates = _router(y1, row(norm_ffn[0]), w_r, b_r)

    pos, src, n_tiles, tile_start, tile_rows = _routing(idx[:, :TOP_K])
    xs = _gather(tile_rows, src, h)
    hs = _expert_matmul(n_tiles, tile_start, xs, w_gate_up.reshape(N_EXPERTS, D_MODEL, 2 * D_FF),
                        b_gate_up.reshape(N_EXPERTS, 1, 2 * D_FF),
                        parts=2, epilogue=_swiglu, out_dtype=BF16, name="expert_gate_up")
    ys = _expert_matmul(n_tiles, tile_start, hs, w_down.reshape(N_EXPERTS, D_FF, D_MODEL),
                        b_down.reshape(N_EXPERTS, 1, D_MODEL),
                        parts=1, epilogue=lambda a: a, out_dtype=F32, name="expert_down")
    y_p = _combine(pos, y1, gates, row(norm_final), ys, 0, N_PROMPT, "combine_final_norm_prompt")
    y_s = _combine(pos, y1, gates, row(norm_final), ys, N_PROMPT, N_SAMPLE, "combine_final_norm_sample")

    keep = min(BAND_ROWS, SEQ)
    shape_p = (1, 1, keep, N_HEADS, HEAD_DIM)
    shape_s = (1, DEC_BATCH, DEC_SEQ, N_HEADS, HEAD_DIM)
    z_s = z[N_PROMPT:].reshape(DEC_BATCH, DEC_SEQ, MIX_CONV)
    return (y_p.reshape(1, SEQ, D_MODEL),
            y_s.reshape(DEC_BATCH, DEC_SEQ, D_MODEL),
            p_p[N_PROMPT - keep:, MIX_ATT:2 * MIX_ATT].reshape(shape_p),
            p_p[N_PROMPT - keep:, 2 * MIX_ATT:3 * MIX_ATT].reshape(shape_p),
            z[N_PROMPT - 2:N_PROMPT].reshape(1, 1, 2, MIX_CONV),
            p_s[:, MIX_ATT:2 * MIX_ATT].reshape(shape_s),
            p_s[:, 2 * MIX_ATT:3 * MIX_ATT].reshape(shape_s),
            z_s[:, DEC_SEQ - 2:].reshape(1, DEC_BATCH, 2, MIX_CONV))
```
